```python
import jax, jax.numpy as jnp
from jax import lax
import numpy as np

D_MODEL = 1024
BATCH = 2
SEQ = 8192
DEPTH = 2

HEAD_DIM = 64
ROPE_THETA = 10000.0
NORM_EPS = 1e-6
A_HEADS = 8
IDX_HEADS = 4
IDX_DIM = 64
IDX_TOPK_MAX = 256
B_HEADS = 8
B_KV_HEADS = 2
WINDOW = 128
C_HEADS = 8
MOBA_BLOCK = 256
MOBA_TOPK = 3
C_Q_CHUNK = 64
Q_BLOCK = 128
N_BRANCH = 3
BRANCH_WIDTH = 8 * HEAD_DIM
D_FF = 4 * D_MODEL
IN_SIZES = (
    A_HEADS * HEAD_DIM, HEAD_DIM, HEAD_DIM,
    IDX_HEADS * IDX_DIM, IDX_DIM, IDX_HEADS,
    B_HEADS * HEAD_DIM, B_KV_HEADS * HEAD_DIM, B_KV_HEADS * HEAD_DIM,
    C_HEADS * HEAD_DIM, C_HEADS * HEAD_DIM, C_HEADS * HEAD_DIM,
    N_BRANCH * D_MODEL,
)
N_IN = sum(IN_SIZES)

kernel_name = 'hybrid_dsa_swa_moba_gated_block'


def rms_norm(x, g):
    xf = x.astype(jnp.float32)
    y = xf * lax.rsqrt(jnp.mean(xf * xf, axis=-1, keepdims=True) + NORM_EPS)
    return (y * g.astype(jnp.float32)).astype(x.dtype)


def rope_tables(seq, dim):
    inv = 1.0 / (ROPE_THETA ** (jnp.arange(0, dim, 2, dtype=jnp.float32) / dim))
    ang = jnp.arange(seq, dtype=jnp.float32)[:, None] * inv[None, :]
    return jnp.cos(ang), jnp.sin(ang)


def apply_rope(x, cos, sin):
    half = x.shape[-1] // 2
    x1, x2 = x[..., :half], x[..., half:]
    c = cos[None, :, None, :].astype(x.dtype)
    s = sin[None, :, None, :].astype(x.dtype)
    return jnp.concatenate([x1 * c - x2 * s, x1 * s + x2 * c], axis=-1)


def dsa_attention(q, k, v, q_idx, k_idx, w_idx):
    b, s, h, d = q.shape
    top_k = min(IDX_TOPK_MAX, s // 4)
    key_pos = jnp.arange(s)
    idx_scale = (IDX_DIM ** -0.5) * (IDX_HEADS ** -0.5)
    att_scale = d ** -0.5

    def one_block(i):
        start = i * Q_BLOCK
        qb = lax.dynamic_slice_in_dim(q, start, Q_BLOCK, axis=1)
        qib = lax.dynamic_slice_in_dim(q_idx, start, Q_BLOCK, axis=1)
        wb = lax.dynamic_slice_in_dim(w_idx, start, Q_BLOCK, axis=1)
        qpos = start + jnp.arange(Q_BLOCK)
        dots = jnp.einsum('bqhe,bse->bqhs', qib, k_idx).astype(jnp.float32)
        score = jnp.einsum('bqh,bqhs->bqs', wb.astype(jnp.float32), jax.nn.relu(dots)) * idx_scale
        causal = key_pos[None, :] <= qpos[:, None]
        score = jnp.where(causal[None], score, -jnp.inf)
        _, sel = lax.top_k(score, top_k)
        valid = sel <= qpos[None, :, None]
        k_sel = jax.vmap(lambda kk, ii: kk[ii])(k, sel)
        v_sel = jax.vmap(lambda vv, ii: vv[ii])(v, sel)
        logits = jnp.einsum('bqhd,bqkd->bqhk', qb, k_sel).astype(jnp.float32) * att_scale
        logits = jnp.where(valid[:, :, None, :], logits, -jnp.inf)
        p = jax.nn.softmax(logits, axis=-1).astype(v.dtype)
        return jnp.einsum('bqhk,bqkd->bqhd', p, v_sel)

    out = lax.map(one_block, jnp.arange(s // Q_BLOCK))
    return out.transpose(1, 0, 2, 3, 4).reshape(b, s, h * d)


def swa_sink_attention(q, k, v, sinks):
    b, s, hq, d = q.shape
    hkv = k.shape[2]
    g = hq // hkv
    nq = s // WINDOW
    qb = q.reshape(b, nq, WINDOW, hkv, g, d)
    pad = ((0, 0), (WINDOW, 0), (0, 0), (0, 0))
    kp = jnp.pad(k, pad).reshape(b, nq + 1, WINDOW, hkv, d)
    vp = jnp.pad(v, pad).reshape(b, nq + 1, WINDOW, hkv, d)
    kb = jnp.concatenate([kp[:, :-1], kp[:, 1:]], axis=2)
    vb = jnp.concatenate([vp[:, :-1], vp[:, 1:]], axis=2)
    logits = jnp.einsum('bnqhgd,bnkhd->bnhgqk', qb, kb).astype(jnp.float32) * (d ** -0.5)
    qpos = jnp.arange(nq)[:, None] * WINDOW + jnp.arange(WINDOW)[None, :]
    kpos = jnp.arange(nq)[:, None] * WINDOW - WINDOW + jnp.arange(2 * WINDOW)[None, :]
    diff = qpos[:, :, None] - kpos[:, None, :]
    mask = (diff >= 0) & (diff < WINDOW) & (kpos[:, None, :] >= 0)
    logits = jnp.where(mask[None, :, None, None], logits, -jnp.inf)
    sink = sinks.astype(jnp.float32).reshape(hkv, g)[None, None, :, :, None, None]
    sink = jnp.broadcast_to(sink, logits.shape[:-1] + (1,))
    p = jax.nn.softmax(jnp.concatenate([logits, sink], axis=-1), axis=-1)[..., :-1]
    out = jnp.einsum('bnhgqk,bnkhd->bnqhgd', p.astype(v.dtype), vb)
    return out.reshape(b, s, hq * d)


def moba_attention(q, k, v):
    b, s, h, d = q.shape
    nkb = -(-s // MOBA_BLOCK)
    pad = nkb * MOBA_BLOCK - s
    padw = ((0, 0), (0, pad), (0, 0), (0, 0))
    kbl = jnp.pad(k, padw).reshape(b, nkb, MOBA_BLOCK, h, d).transpose(0, 3, 1, 2, 4)
    vbl = jnp.pad(v, padw).reshape(b, nkb, MOBA_BLOCK, h, d).transpose(0, 3, 1, 2, 4)
    kmean = jnp.mean(kbl.astype(jnp.float32), axis=3)
    n_sel = min(MOBA_TOPK, nkb - 1)
    scale = d ** -0.5
    bi = jnp.arange(b)[:, None, None, None]
    hi = jnp.arange(h)[None, :, None, None]
    blk_ids = jnp.arange(nkb)

    def one_chunk(c):
        start = c * C_Q_CHUNK
        qc = lax.dynamic_slice_in_dim(q, start, C_Q_CHUNK, axis=1).transpose(0, 2, 1, 3)
        qpos = start + jnp.arange(C_Q_CHUNK)
        own = start // MOBA_BLOCK
        k_own = lax.dynamic_index_in_dim(kbl, own, axis=2, keepdims=False)
        v_own = lax.dynamic_index_in_dim(vbl, own, axis=2, keepdims=False)
        kpos_own = own * MOBA_BLOCK + jnp.arange(MOBA_BLOCK)
        l_own = jnp.einsum('bhqd,bhkd->bhqk', qc, k_own).astype(jnp.float32) * scale
        l_own = jnp.where((kpos_own[None, :] <= qpos[:, None])[None, None], l_own, -jnp.inf)
        if n_sel == 0:
            p_own = jax.nn.softmax(l_own, axis=-1).astype(v.dtype)
            out = jnp.einsum('bhqk,bhkd->bhqd', p_own, v_own)
            return out.transpose(0, 2, 1, 3)
        gate = jnp.einsum('bhqd,bhnd->bhqn', qc.astype(jnp.float32), kmean)
        gate = jnp.where((blk_ids < own)[None, None, None, :], gate, -jnp.inf)
        _, sel = lax.top_k(gate, n_sel)
        valid = sel < own
        k_sel = kbl[bi, hi, sel]
        v_sel = vbl[bi, hi, sel]
        l_sel = jnp.einsum('bhqd,bhqnkd->bhqnk', qc, k_sel).astype(jnp.float32) * scale
        l_sel = jnp.where(valid[..., None], l_sel, -jnp.inf).reshape(b, h, C_Q_CHUNK, n_sel * MOBA_BLOCK)
        p = jax.nn.softmax(jnp.concatenate([l_sel, l_own], axis=-1), axis=-1).astype(v.dtype)
        p_sel = p[..., :n_sel * MOBA_BLOCK].reshape(b, h, C_Q_CHUNK, n_sel, MOBA_BLOCK)
        p_own = p[..., n_sel * MOBA_BLOCK:]
        out = (jnp.einsum('bhqnk,bhqnkd->bhqd', p_sel, v_sel)
               + jnp.einsum('bhqk,bhkd->bhqd', p_own, v_own))
        return out.transpose(0, 2, 1, 3)

    out = lax.map(one_chunk, jnp.arange(s // C_Q_CHUNK))
    return out.transpose(1, 0, 2, 3, 4).reshape(b, s, h * d)


def hybrid_layer(x, cos, sin, norm1, w_in, sinks, w_branch, w_out, norm2, w_ff1, w_ff2):
    b, s, _ = x.shape
    h = rms_norm(x, norm1)
    proj = h @ w_in
    split_points = np.cumsum(np.array(IN_SIZES))[:-1].tolist()
    (aq, ak, av, iq, ik, iw, bq, bk, bv, cq, ck, cv, gates) = jnp.split(proj, split_points, axis=-1)

    aq = apply_rope(aq.reshape(b, s, A_HEADS, HEAD_DIM), cos, sin)
    ak = apply_rope(ak[:, :, None, :], cos, sin)[:, :, 0, :]
    iq = apply_rope(iq.reshape(b, s, IDX_HEADS, IDX_DIM), cos, sin)
    ik = apply_rope(ik[:, :, None, :], cos, sin)[:, :, 0, :]
    o_a = dsa_attention(aq, ak, av, iq, ik, iw)

    bq = apply_rope(bq.reshape(b, s, B_HEADS, HEAD_DIM), cos, sin)
    bk = apply_rope(bk.reshape(b, s, B_KV_HEADS, HEAD_DIM), cos, sin)
    bv = bv.reshape(b, s, B_KV_HEADS, HEAD_DIM)
    o_b = swa_sink_attention(bq, bk, bv, sinks)

    cq = apply_rope(cq.reshape(b, s, C_HEADS, HEAD_DIM), cos, sin)
    ck = apply_rope(ck.reshape(b, s, C_HEADS, HEAD_DIM), cos, sin)
    cv = cv.reshape(b, s, C_HEADS, HEAD_DIM)
    o_c = moba_attention(cq, ck, cv)

    branches = jnp.stack([o_a, o_b, o_c], axis=2)
    g = jax.nn.sigmoid(gates.reshape(b, s, N_BRANCH, D_MODEL))
    merged = jnp.sum(g * jnp.einsum('bsnc,ncd->bsnd', branches, w_branch), axis=2)
    x = x + merged @ w_out

    h2 = rms_norm(x, norm2)
    x = x + jnp.square(jax.nn.relu(h2 @ w_ff1)) @ w_ff2
    return x


def setup_inputs(seed: int = 0) -> dict:
    key = jax.random.key(seed)
    ks = jax.random.split(key, 10)
    f32 = jnp.float32
    x = jax.random.normal(ks[0], (BATCH, SEQ, D_MODEL), f32)
    norm1 = 1.0 + 0.01 * jax.random.normal(ks[1], (DEPTH, D_MODEL), f32)
    w_in = jax.random.normal(ks[2], (DEPTH, D_MODEL, N_IN), f32) * D_MODEL ** -0.5
    sinks = 0.5 * jax.random.normal(ks[3], (DEPTH, B_HEADS), f32)
    w_branch = jax.random.normal(ks[4], (DEPTH, N_BRANCH, BRANCH_WIDTH, D_MODEL), f32) * BRANCH_WIDTH ** -0.5
    w_out = jax.random.normal(ks[5], (DEPTH, D_MODEL, D_MODEL), f32) * D_MODEL ** -0.5
    norm2 = 1.0 + 0.01 * jax.random.normal(ks[6], (DEPTH, D_MODEL), f32)
    w_ff1 = jax.random.normal(ks[7], (DEPTH, D_MODEL, D_FF), f32) * D_MODEL ** -0.5
    w_ff2 = jax.random.normal(ks[8], (DEPTH, D_FF, D_MODEL), f32) * D_FF ** -0.5
    norm_f = 1.0 + 0.01 * jax.random.normal(ks[9], (D_MODEL,), f32)
    return {'x': x, 'norm1': norm1, 'w_in': w_in, 'sinks': sinks, 'w_branch': w_branch,
            'w_out': w_out, 'norm2': norm2, 'w_ff1': w_ff1, 'w_ff2': w_ff2, 'norm_f': norm_f}


def reference(x, norm1, w_in, sinks, w_branch, w_out, norm2, w_ff1, w_ff2, norm_f):
    cos, sin = rope_tables(x.shape[1], HEAD_DIM)
    for l in range(DEPTH):
        x = hybrid_layer(x, cos, sin, norm1[l], w_in[l], sinks[l], w_branch[l], w_out[l],
                         norm2[l], w_ff1[l], w_ff2[l])
    return rms_norm(x, norm_f)
```

```python
import functools

import jax
import jax.numpy as jnp
import numpy as np
from jax import lax
from jax.experimental import pallas as pl
from jax.experimental.pallas import tpu as pltpu

F32 = jnp.float32
BF16 = jnp.bfloat16
I32 = jnp.int32

D_MODEL = 1024
HEAD_DIM = 64
ROPE_THETA = 10000.0
NORM_EPS = 1e-6
A_HEADS = 8
IDX_HEADS = 4
IDX_DIM = 64
IDX_TOPK_MAX = 256
B_HEADS = 8
B_KV_HEADS = 2
WINDOW = 128
C_HEADS = 8
MOBA_BLOCK = 256
MOBA_TOPK = 3
Q_BLOCK = 128
N_BRANCH = 3
BRANCH_WIDTH = 8 * HEAD_DIM
D_FF = 4 * D_MODEL

LANES = 128
NEG_BIG = -1e30
INT_MIN = -(2 ** 31)

COL_GATES = 0
COL_AQ = 3072
COL_BQ = 3584
COL_CQ = 4096
COL_CK = 4608
COL_IQ = 5120
COL_BK = 5376
COL_AKIK = 5504
ROPE_BEGIN, ROPE_END = 3072, 5632
COL_AVIW = 5632
COL_BV = 5760
COL_CV = 5888
N_PACKED = 6656

VMEM_LIMIT = 56 * 1024 * 1024


def _cparams(n_axes):
    return pltpu.CompilerParams(dimension_semantics=("arbitrary",) * n_axes,
                                vmem_limit_bytes=VMEM_LIMIT)


def _nt_dot(a, b):
    return lax.dot_general(a, b, (((1,), (1,)), ((), ())), preferred_element_type=F32)


def _split3_lhs(x):
    hi = x.astype(BF16)
    lo = (x - hi.astype(F32)).astype(BF16)
    return jnp.concatenate([hi, hi, lo], axis=1)


def _split3_rhs(x):
    hi = x.astype(BF16)
    lo = (x - hi.astype(F32)).astype(BF16)
    return jnp.concatenate([hi, lo, hi], axis=1)


def _proj_kernel(x_ref, g_ref, w_ref, cos_ref, sin_ref, o_ref, h_ref, *, rope_lo, rope_hi, tn):
    j = pl.program_id(1)

    @pl.when(j == 0)
    def _():
        x = x_ref[...]
        ms = jnp.mean(x * x, axis=-1, keepdims=True)
        h_ref[...] = (x * lax.rsqrt(ms + NORM_EPS) * g_ref[...]).astype(BF16)

    acc = jnp.dot(h_ref[...], w_ref[...], preferred_element_type=F32)
    is_rope = jnp.logical_and(j >= rope_lo, j < rope_hi)

    @pl.when(is_rope)
    def _():
        cos = cos_ref[...]
        sin = sin_ref[...]
        lane = lax.broadcasted_iota(I32, cos.shape, 1)
        first_half = (lane % HEAD_DIM) < (HEAD_DIM // 2)
        for c in range(tn // LANES):
            a = acc[:, c * LANES:(c + 1) * LANES]
            partner = jnp.where(first_half,
                                pltpu.roll(a, LANES - HEAD_DIM // 2, 1),
                                pltpu.roll(a, HEAD_DIM // 2, 1))
            o_ref[:, c * LANES:(c + 1) * LANES] = a * cos + partner * sin

    @pl.when(jnp.logical_not(is_rope))
    def _():
        o_ref[...] = acc


def _project(x2, gain, w_packed, cos_t, sin_t, seq, tm=512, tn=512):
    t = x2.shape[0]
    n = w_packed.shape[1]
    assert t % tm == 0 and seq % tm == 0 and n % tn == 0
    assert ROPE_BEGIN % tn == 0 and ROPE_END % tn == 0
    pos_blocks = seq // tm
    kern = functools.partial(_proj_kernel, rope_lo=ROPE_BEGIN // tn, rope_hi=ROPE_END // tn, tn=tn)
    return pl.pallas_call(
        kern,
        grid=(t // tm, n // tn),
        in_specs=[
            pl.BlockSpec((tm, D_MODEL), lambda i, j: (i, 0)),
            pl.BlockSpec((1, D_MODEL), lambda i, j: (0, 0)),
            pl.BlockSpec((D_MODEL, tn), lambda i, j: (0, j)),
            pl.BlockSpec((tm, LANES), lambda i, j: (i % pos_blocks, 0)),
            pl.BlockSpec((tm, LANES), lambda i, j: (i % pos_blocks, 0)),
        ],
        out_specs=pl.BlockSpec((tm, tn), lambda i, j: (i, j)),
        out_shape=jax.ShapeDtypeStruct((t, n), F32),
        scratch_shapes=[pltpu.VMEM((tm, D_MODEL), BF16)],
        compiler_params=_cparams(2),
        name="norm_proj_rope",
    )(x2, gain, w_packed, cos_t, sin_t)


def _dsa_kernel(aq_ref, iq_ref, kk_ref, vw_ref, o_ref,
                kbf_ref, vbf_ref, ikc_ref, keys_ref, qall_ref, m_ref, l_ref, acc_ref,
                *, seq, kc, top_k, idx_bits):
    i = pl.program_id(1)
    qb = Q_BLOCK
    n_prep = seq // kc
    idx_scale = (IDX_DIM ** -0.5) * (IDX_HEADS ** -0.5)

    @pl.when(i == 0)
    def _prep():
        def body(c, carry):
            rows = pl.ds(pl.multiple_of(c * kc, kc), kc)
            kk = kk_ref[rows, :]
            kbf_ref[rows, :] = kk[:, :HEAD_DIM].astype(BF16)
            ikc_ref[rows, :] = _split3_rhs(kk[:, HEAD_DIM:])
            vbf_ref[rows, :] = vw_ref[rows, :HEAD_DIM].astype(BF16)
            return carry
        lax.fori_loop(0, n_prep, body, 0)

    nkc = (i * qb) // kc + 1

    iq = iq_ref[...]
    w = vw_ref[pl.ds(pl.multiple_of(i * qb, qb), qb), HEAD_DIM:HEAD_DIM + IDX_HEADS] * idx_scale
    qcat = jnp.concatenate(
        [_split3_lhs(iq[:, h * IDX_DIM:(h + 1) * IDX_DIM]) for h in range(IDX_HEADS)], axis=0)
    row = lax.broadcasted_iota(I32, (qb, kc), 0)
    col = lax.broadcasted_iota(I32, (qb, kc), 1)
    qpos = i * qb + row

    def score_body(c, carry):
        rows = pl.ds(pl.multiple_of(c * kc, kc), kc)
        dots = _nt_dot(qcat, ikc_ref[rows, :])
        sc = jnp.zeros((qb, kc), F32)
        for h in range(IDX_HEADS):
            sc = sc + jnp.maximum(dots[h * qb:(h + 1) * qb], 0.0) * w[:, h:h + 1]
        sc = jnp.where(sc == 0.0, 0.0, sc)
        bits = lax.bitcast_convert_type(sc, I32)
        key = jnp.where(bits < 0, bits ^ jnp.int32(0x7FFFFFFF), bits)
        keys_ref[c] = jnp.where(c * kc + col <= qpos, key, jnp.int32(INT_MIN))
        return carry
    lax.fori_loop(0, nkc, score_body, 0)

    def count(pred, thr):
        thr_b = jnp.broadcast_to(thr, (qb, LANES))

        def body(c, acc):
            kch = keys_ref[c]
            for g in range(kc // LANES):
                acc = acc + jnp.where(pred(kch[:, g * LANES:(g + 1) * LANES], thr_b, c, g), 1.0, 0.0)
            return acc
        acc = lax.fori_loop(0, nkc, body, jnp.zeros((qb, LANES), F32))
        return jnp.sum(acc, axis=1, keepdims=True)

    ge = lambda k, t, c, g: k >= t
    kf = float(top_k)
    t0 = jnp.where(count(ge, jnp.zeros((qb, 1), I32)) >= kf, jnp.int32(0), jnp.int32(INT_MIN))

    def bit_body(p, t):
        cand = t | jnp.left_shift(jnp.int32(1), 30 - p)
        return jnp.where(count(ge, cand) >= kf, cand, t)
    thr = lax.fori_loop(0, 31, bit_body, t0)

    real = thr > jnp.int32(INT_MIN)
    n_ge = count(ge, thr)
    excess = jnp.max(jnp.where(jnp.logical_and(real, n_ge > kf), 1.0, 0.0)) > 0.0

    def tie_search():
        n_gt = count(lambda k, t, c, g: k > t, thr)
        need = kf - n_gt
        lane = lax.broadcasted_iota(I32, (qb, LANES), 1)

        def ties_below(jc):
            jc_b = jnp.broadcast_to(jc, (qb, LANES))
            thr_b = jnp.broadcast_to(thr, (qb, LANES))

            def body(c, acc):
                kch = keys_ref[c]
                for g in range(kc // LANES):
                    idx = c * kc + g * LANES + lane
                    hit = jnp.logical_and(kch[:, g * LANES:(g + 1) * LANES] == thr_b, idx < jc_b)
                    acc = acc + jnp.where(hit, 1.0, 0.0)
                return acc
            acc = lax.fori_loop(0, nkc, body, jnp.zeros((qb, LANES), F32))
            return jnp.sum(acc, axis=1, keepdims=True)

        def jbody(p, jcur):
            cand = jcur | jnp.left_shift(jnp.int32(1), idx_bits - 1 - p)
            return jnp.where(ties_below(cand) < need, cand, jcur)
        return lax.fori_loop(0, idx_bits, jbody, jnp.zeros((qb, 1), I32))

    jlast = lax.cond(excess, tie_search, lambda: jnp.full((qb, 1), seq, I32))
    jlast = jnp.where(real, jlast, jnp.int32(-1))

    aq = aq_ref[...]
    for h in range(A_HEADS):
        qall_ref[h * qb:(h + 1) * qb, :] = (
            aq[:, h * HEAD_DIM:(h + 1) * HEAD_DIM] * (HEAD_DIM ** -0.5)).astype(BF16)
    m_ref[...] = jnp.full(m_ref.shape, NEG_BIG, F32)
    l_ref[...] = jnp.zeros(l_ref.shape, F32)
    acc_ref[...] = jnp.zeros(acc_ref.shape, F32)
    thr_b = jnp.broadcast_to(thr, (qb, kc))
    jl_b = jnp.broadcast_to(jlast, (qb, kc))

    def att_body(c, carry):
        rows = pl.ds(pl.multiple_of(c * kc, kc), kc)
        kch = keys_ref[c]
        sel = jnp.logical_or(kch > thr_b,
                             jnp.logical_and(kch == thr_b, c * kc + col <= jl_b))
        s = _nt_dot(qall_ref[...], kbf_ref[rows, :])
        s = jnp.where(sel[None], s.reshape(A_HEADS, qb, kc), NEG_BIG).reshape(A_HEADS * qb, kc)
        m_old = m_ref[...]
        m_new = jnp.maximum(m_old, jnp.max(s, axis=1, keepdims=True))
        alpha = jnp.exp(m_old - m_new)
        p = jnp.exp(s - m_new)
        l_ref[...] = alpha * l_ref[...] + jnp.sum(p, axis=1, keepdims=True)
        acc_ref[...] = alpha * acc_ref[...] + jnp.dot(
            p.astype(BF16), vbf_ref[rows, :], preferred_element_type=F32)
        m_ref[...] = m_new
        return carry
    lax.fori_loop(0, nkc, att_body, 0)

    out = acc_ref[...] / l_ref[...]
    for h in range(A_HEADS):
        o_ref[:, h * HEAD_DIM:(h + 1) * HEAD_DIM] = out[h * qb:(h + 1) * qb].astype(o_ref.dtype)


def _dsa(proj3, kc=512):
    b, seq, _ = proj3.shape
    top_k = min(IDX_TOPK_MAX, seq // 4)
    kc = min(kc, seq)
    assert seq % kc == 0 and kc % Q_BLOCK == 0 and kc >= top_k
    idx_bits = max(1, int(np.ceil(np.log2(seq))))
    kern = functools.partial(_dsa_kernel, seq=seq, kc=kc, top_k=top_k, idx_bits=idx_bits)
    hq = A_HEADS * Q_BLOCK
    return pl.pallas_call(
        kern,
        grid=(b, seq // Q_BLOCK),
        in_specs=[
            pl.BlockSpec((None, Q_BLOCK, A_HEADS * HEAD_DIM), lambda bb, i: (bb, i, COL_AQ // 512)),
            pl.BlockSpec((None, Q_BLOCK, IDX_HEADS * IDX_DIM), lambda bb, i: (bb, i, COL_IQ // 256)),
            pl.BlockSpec((None, seq, LANES), lambda bb, i: (bb, 0, COL_AKIK // LANES)),
            pl.BlockSpec((None, seq, LANES), lambda bb, i: (bb, 0, COL_AVIW // LANES)),
        ],
        out_specs=pl.BlockSpec((None, Q_BLOCK, A_HEADS * HEAD_DIM), lambda bb, i: (bb, i, 0)),
        out_shape=jax.ShapeDtypeStruct((b, seq, A_HEADS * HEAD_DIM), BF16),
        scratch_shapes=[
            pltpu.VMEM((seq, HEAD_DIM), BF16),
            pltpu.VMEM((seq, HEAD_DIM), BF16),
            pltpu.VMEM((seq, 3 * IDX_DIM), BF16),
            pltpu.VMEM((seq // kc, Q_BLOCK, kc), I32),
            pltpu.VMEM((hq, HEAD_DIM), BF16),
            pltpu.VMEM((hq, 1), F32),
            pltpu.VMEM((hq, 1), F32),
            pltpu.VMEM((hq, HEAD_DIM), F32),
        ],
        compiler_params=_cparams(2),
        name="dsa_topk_attention",
    )(proj3, proj3, proj3, proj3)


def _swa_kernel(sink_ref, q_ref, kp_ref, kc_ref, vp_ref, vc_ref, o_ref):
    i = pl.program_id(1)
    w = WINDOW
    group = B_HEADS // B_KV_HEADS
    q = q_ref[...]
    k = jnp.concatenate([kp_ref[...], kc_ref[...]], axis=0)
    v = jnp.concatenate([vp_ref[...], vc_ref[...]], axis=0)
    row = lax.broadcasted_iota(I32, (w, 2 * w), 0)
    col = lax.broadcasted_iota(I32, (w, 2 * w), 1)
    kpos = (i - 1) * w + col
    diff = i * w + row - kpos
    mask = jnp.logical_and(jnp.logical_and(diff >= 0, diff < w), kpos >= 0)
    for g in range(B_KV_HEADS):
        kg = k[:, g * HEAD_DIM:(g + 1) * HEAD_DIM].astype(BF16)
        vg = v[:, g * HEAD_DIM:(g + 1) * HEAD_DIM].astype(BF16)
        for hh in range(group):
            h = g * group + hh
            qh = (q[:, h * HEAD_DIM:(h + 1) * HEAD_DIM] * (HEAD_DIM ** -0.5)).astype(BF16)
            s = jnp.where(mask, _nt_dot(qh, kg), NEG_BIG)
            sink = sink_ref[h]
            m = jnp.maximum(jnp.max(s, axis=1, keepdims=True), sink)
            p = jnp.exp(s - m)
            den = jnp.sum(p, axis=1, keepdims=True) + jnp.exp(sink - m)
            o = jnp.dot(p.astype(BF16), vg, preferred_element_type=F32) / den
            o_ref[:, h * HEAD_DIM:(h + 1) * HEAD_DIM] = o.astype(o_ref.dtype)


def _swa(proj3, sinks):
    b, seq, _ = proj3.shape
    w = WINDOW
    kvw = B_KV_HEADS * HEAD_DIM
    assert kvw == LANES and seq % w == 0
    prev = lambda col: (lambda bb, i: (bb, jnp.maximum(i - 1, 0), col))
    cur = lambda col: (lambda bb, i: (bb, i, col))
    return pl.pallas_call(
        _swa_kernel,
        grid=(b, seq // w),
        in_specs=[
            pl.BlockSpec(memory_space=pltpu.SMEM),
            pl.BlockSpec((None, w, B_HEADS * HEAD_DIM), cur(COL_BQ // 512)),
            pl.BlockSpec((None, w, kvw), prev(COL_BK // LANES)),
            pl.BlockSpec((None, w, kvw), cur(COL_BK // LANES)),
            pl.BlockSpec((None, w, kvw), prev(COL_BV // LANES)),
            pl.BlockSpec((None, w, kvw), cur(COL_BV // LANES)),
        ],
        out_specs=pl.BlockSpec((None, w, B_HEADS * HEAD_DIM), lambda bb, i: (bb, i, 0)),
        out_shape=jax.ShapeDtypeStruct((b, seq, B_HEADS * HEAD_DIM), BF16),
        compiler_params=_cparams(2),
        name="swa_sink_attention",
    )(sinks, proj3, proj3, proj3, proj3, proj3)


def _moba_kernel(q_ref, k_ref, v_ref, o_ref,
                 kbf_ref, vbf_ref, kmean_ref, kmc_ref, m_ref, l_ref, acc_ref,
                 *, seq, tq, n_sel):
    i = pl.program_id(2)
    blk = MOBA_BLOCK
    nkb = seq // blk
    heads = LANES // HEAD_DIM

    @pl.when(i == 0)
    def _prep():
        kmean_ref[...] = jnp.zeros(kmean_ref.shape, F32)

        def body(n, carry):
            rows = pl.ds(pl.multiple_of(n * blk, blk), blk)
            kb = k_ref[rows, :]
            vb = v_ref[rows, :]
            for e in range(heads):
                kbf_ref[e, rows, :] = kb[:, e * HEAD_DIM:(e + 1) * HEAD_DIM].astype(BF16)
                vbf_ref[e, rows, :] = vb[:, e * HEAD_DIM:(e + 1) * HEAD_DIM].astype(BF16)
            kmean_ref[pl.ds(n, 1), :] = jnp.sum(kb, axis=0, keepdims=True) * (1.0 / blk)
            return carry
        lax.fori_loop(0, nkb, body, 0)
        km = kmean_ref[...]
        for e in range(heads):
            kmc_ref[e] = _split3_rhs(km[:, e * HEAD_DIM:(e + 1) * HEAD_DIM])

    q = q_ref[...]
    qpos1 = i * tq + lax.broadcasted_iota(I32, (tq, 1), 0)
    own1 = qpos1 // blk
    lane = lax.broadcasted_iota(I32, (tq, LANES), 1)
    col = lax.broadcasted_iota(I32, (tq, blk), 1)
    qpos = i * tq + lax.broadcasted_iota(I32, (tq, blk), 0)
    own = qpos // blk
    first_own = (i * tq) // blk

    for e in range(heads):
        qe = q[:, e * HEAD_DIM:(e + 1) * HEAD_DIM]
        gate = _nt_dot(_split3_lhs(qe), kmc_ref[e])
        g = jnp.where(lane < own1, gate, -jnp.inf)
        selm = jnp.zeros((tq, LANES), F32)
        for _ in range(n_sel):
            gmax = jnp.max(g, axis=1, keepdims=True)
            is_max = jnp.logical_and(g == gmax, gmax > -jnp.inf)
            first = jnp.min(jnp.where(is_max, lane, LANES), axis=1, keepdims=True)
            onehot = lane == first
            selm = jnp.where(onehot, 1.0, selm)
            g = jnp.where(onehot, -jnp.inf, g)

        qbf = (qe * (HEAD_DIM ** -0.5)).astype(BF16)
        m_ref[...] = jnp.full(m_ref.shape, NEG_BIG, F32)
        l_ref[...] = jnp.zeros(l_ref.shape, F32)
        acc_ref[...] = jnp.zeros(acc_ref.shape, F32)

        def flash(jb, mask):
            rows = pl.ds(pl.multiple_of(jb * blk, blk), blk)
            s = jnp.where(mask, _nt_dot(qbf, kbf_ref[e, rows, :]), NEG_BIG)
            m_old = m_ref[...]
            m_new = jnp.maximum(m_old, jnp.max(s, axis=1, keepdims=True))
            alpha = jnp.exp(m_old - m_new)
            p = jnp.exp(s - m_new)
            l_ref[...] = alpha * l_ref[...] + jnp.sum(p, axis=1, keepdims=True)
            acc_ref[...] = alpha * acc_ref[...] + jnp.dot(
                p.astype(BF16), vbf_ref[e, rows, :], preferred_element_type=F32)
            m_ref[...] = m_new

        def picked(jb):
            return jnp.sum(jnp.where(lane == jb, selm, 0.0), axis=1, keepdims=True) > 0.0

        def past_body(jb, carry):
            flash(jb, jnp.broadcast_to(picked(jb), (tq, blk)))
            return carry
        lax.fori_loop(0, first_own, past_body, 0)

        for d in range(tq // blk):
            jb = first_own + d
            causal = jnp.logical_and(own == jb, jb * blk + col <= qpos)
            flash(jb, jnp.logical_or(causal, picked(jb)))

        o_ref[:, e * HEAD_DIM:(e + 1) * HEAD_DIM] = (acc_ref[...] / l_ref[...]).astype(o_ref.dtype)


def _moba(proj3, tq=512):
    b, seq, _ = proj3.shape
    tq = min(tq, seq)
    blk = MOBA_BLOCK
    assert seq % blk == 0 and tq % blk == 0 and seq % tq == 0
    nkb = seq // blk
    assert nkb <= LANES
    n_sel = min(MOBA_TOPK, nkb - 1)
    heads = LANES // HEAD_DIM
    kern = functools.partial(_moba_kernel, seq=seq, tq=tq, n_sel=n_sel)
    return pl.pallas_call(
        kern,
        grid=(b, C_HEADS // heads, seq // tq),
        in_specs=[
            pl.BlockSpec((None, tq, LANES), lambda bb, hp, i: (bb, i, COL_CQ // LANES + hp)),
            pl.BlockSpec((None, seq, LANES), lambda bb, hp, i: (bb, 0, COL_CK // LANES + hp)),
            pl.BlockSpec((None, seq, LANES), lambda bb, hp, i: (bb, 0, COL_CV // LANES + hp)),
        ],
        out_specs=pl.BlockSpec((None, tq, LANES), lambda bb, hp, i: (bb, i, hp)),
        out_shape=jax.ShapeDtypeStruct((b, seq, C_HEADS * HEAD_DIM), BF16),
        scratch_shapes=[
            pltpu.VMEM((heads, seq, HEAD_DIM), BF16),
            pltpu.VMEM((heads, seq, HEAD_DIM), BF16),
            pltpu.VMEM((LANES, LANES), F32),
            pltpu.VMEM((heads, LANES, 3 * HEAD_DIM), BF16),
            pltpu.VMEM((tq, 1), F32),
            pltpu.VMEM((tq, 1), F32),
            pltpu.VMEM((tq, HEAD_DIM), F32),
        ],
        compiler_params=_cparams(3),
        name="moba_attention",
    )(proj3, proj3, proj3)


def _merge_kernel(x_ref, oa_ref, ob_ref, oc_ref, g_ref, wb_ref, wo_ref, o_ref):
    merged = None
    for n, br in enumerate((oa_ref, ob_ref, oc_ref)):
        y = jnp.dot(br[...], wb_ref[n], preferred_element_type=F32)
        term = jax.nn.sigmoid(g_ref[:, n * D_MODEL:(n + 1) * D_MODEL]) * y
        merged = term if merged is None else merged + term
    o_ref[...] = x_ref[...] + jnp.dot(merged.astype(BF16), wo_ref[...], preferred_element_type=F32)


def _merge(x2, oa, ob, oc, proj2, w_branch, w_out, tm=512):
    t = x2.shape[0]
    assert t % tm == 0
    row = lambda i: (i, 0)
    return pl.pallas_call(
        _merge_kernel,
        grid=(t // tm,),
        in_specs=[
            pl.BlockSpec((tm, D_MODEL), row),
            pl.BlockSpec((tm, BRANCH_WIDTH), row),
            pl.BlockSpec((tm, BRANCH_WIDTH), row),
            pl.BlockSpec((tm, BRANCH_WIDTH), row),
            pl.BlockSpec((tm, N_BRANCH * D_MODEL), lambda i: (i, COL_GATES // (N_BRANCH * D_MODEL))),
            pl.BlockSpec((N_BRANCH, BRANCH_WIDTH, D_MODEL), lambda i: (0, 0, 0)),
            pl.BlockSpec((D_MODEL, D_MODEL), lambda i: (0, 0)),
        ],
        out_specs=pl.BlockSpec((tm, D_MODEL), row),
        out_shape=jax.ShapeDtypeStruct((t, D_MODEL), F32),
        compiler_params=_cparams(1),
        name="gated_merge_out_proj",
    )(x2, oa, ob, oc, proj2, w_branch, w_out)


def _ffn_kernel(x_ref, g_ref, w1_ref, w2_ref, gf_ref, o_ref, h_ref, acc_ref, *, final_norm):
    j = pl.program_id(1)

    @pl.when(j == 0)
    def _():
        x = x_ref[...]
        ms = jnp.mean(x * x, axis=-1, keepdims=True)
        h_ref[...] = (x * lax.rsqrt(ms + NORM_EPS) * g_ref[...]).astype(BF16)
        acc_ref[...] = jnp.zeros(acc_ref.shape, F32)

    u = jnp.maximum(jnp.dot(h_ref[...], w1_ref[...], preferred_element_type=F32), 0.0)
    acc_ref[...] += jnp.dot((u * u).astype(BF16), w2_ref[...], preferred_element_type=F32)

    @pl.when(j == pl.num_programs(1) - 1)
    def _():
        y = x_ref[...] + acc_ref[...]
        if final_norm:
            ms = jnp.mean(y * y, axis=-1, keepdims=True)
            y = y * lax.rsqrt(ms + NORM_EPS) * gf_ref[...]
        o_ref[...] = y


def _ffn(x2, gain, w1, w2, gain_f, final_norm, tm=512, tf=1024):
    t = x2.shape[0]
    assert t % tm == 0 and D_FF % tf == 0
    kern = functools.partial(_ffn_kernel, final_norm=final_norm)
    return pl.pallas_call(
        kern,
        grid=(t // tm, D_FF // tf),
        in_specs=[
            pl.BlockSpec((tm, D_MODEL), lambda i, j: (i, 0)),
            pl.BlockSpec((1, D_MODEL), lambda i, j: (0, 0)),
            pl.BlockSpec((D_MODEL, tf), lambda i, j: (0, j)),
            pl.BlockSpec((tf, D_MODEL), lambda i, j: (j, 0)),
            pl.BlockSpec((1, D_MODEL), lambda i, j: (0, 0)),
        ],
        out_specs=pl.BlockSpec((tm, D_MODEL), lambda i, j: (i, 0)),
        out_shape=jax.ShapeDtypeStruct((t, D_MODEL), F32),
        scratch_shapes=[pltpu.VMEM((tm, D_MODEL), BF16), pltpu.VMEM((tm, D_MODEL), F32)],
        compiler_params=_cparams(2),
        name="relu2_mlp",
    )(x2, gain, w1, w2, gain_f)


def _pack_w_in(w_in):
    sizes = (A_HEADS * HEAD_DIM, HEAD_DIM, HEAD_DIM, IDX_HEADS * IDX_DIM, IDX_DIM, IDX_HEADS,
             B_HEADS * HEAD_DIM, B_KV_HEADS * HEAD_DIM, B_KV_HEADS * HEAD_DIM,
             C_HEADS * HEAD_DIM, C_HEADS * HEAD_DIM, C_HEADS * HEAD_DIM, N_BRANCH * D_MODEL)
    offs = np.concatenate([[0], np.cumsum(sizes)])
    aq, ak, av, iq, ik, iw, bq, bk, bv, cq, ck, cv, gates = [
        w_in[..., offs[n]:offs[n + 1]] for n in range(len(sizes))]
    lead = w_in.shape[:-1]
    pad_iw = jnp.zeros(lead + (LANES - HEAD_DIM - IDX_HEADS,), w_in.dtype)
    parts = [gates, aq, bq, cq, ck, iq, bk, ak, ik, av, iw, pad_iw, bv, cv]
    used = sum(p.shape[-1] for p in parts)
    parts.append(jnp.zeros(lead + (N_PACKED - used,), w_in.dtype))
    return jnp.concatenate(parts, axis=-1).astype(BF16)


def _rope_tables(seq):
    half = HEAD_DIM // 2
    inv = 1.0 / (ROPE_THETA ** (jnp.arange(0, HEAD_DIM, 2, dtype=F32) / HEAD_DIM))
    ang = jnp.arange(seq, dtype=F32)[:, None] * inv[None, :]
    cos, sin = jnp.cos(ang), jnp.sin(ang)
    reps = LANES // HEAD_DIM
    cos_t = jnp.tile(jnp.concatenate([cos, cos], axis=1), (1, reps))
    sin_t = jnp.tile(jnp.concatenate([-sin, sin], axis=1), (1, reps))
    assert cos_t.shape == (seq, LANES) and half * 2 == HEAD_DIM
    return cos_t, sin_t


def kernel(x, norm1, w_in, sinks, w_branch, w_out, norm2, w_ff1, w_ff2, norm_f):
    b, seq, d = x.shape
    depth = w_in.shape[0]
    cos_t, sin_t = _rope_tables(seq)
    w_in_p = _pack_w_in(w_in)
    w_branch_b = w_branch.astype(BF16)
    w_out_b = w_out.astype(BF16)
    w_ff1_b = w_ff1.astype(BF16)
    w_ff2_b = w_ff2.astype(BF16)
    gain_f = norm_f.reshape(1, d)

    x2 = x.reshape(b * seq, d)
    for l in range(depth):
        proj2 = _project(x2, norm1[l].reshape(1, d), w_in_p[l], cos_t, sin_t, seq)
        proj3 = proj2.reshape(b, seq, N_PACKED)
        o_a = _dsa(proj3).reshape(b * seq, BRANCH_WIDTH)
        o_b = _swa(proj3, sinks[l]).reshape(b * seq, BRANCH_WIDTH)
        o_c = _moba(proj3).reshape(b * seq, BRANCH_WIDTH)
        x2 = _merge(x2, o_a, o_b, o_c, proj2, w_branch_b[l], w_out_b[l])
        x2 = _ffn(x2, norm2[l].reshape(1, d), w_ff1_b[l], w_ff2_b[l], gain_f,
                  final_norm=(l == depth - 1))
    return x2.reshape(b, seq, d)
```

```python
import functools

import jax
import jax.numpy as jnp
import numpy as np
from jax import lax
from jax.experimental import pallas as pl
from jax.experimental.pallas import tpu as pltpu

F32 = jnp.float32
BF16 = jnp.bfloat16
I32 = jnp.int32

D_MODEL = 1024
HEAD_DIM = 64
ROPE_THETA = 10000.0
NORM_EPS = 1e-6
A_HEADS = 8
IDX_HEADS = 4
IDX_DIM = 64
IDX_TOPK_MAX = 256
B_HEADS = 8
B_KV_HEADS = 2
WINDOW = 128
C_HEADS = 8
MOBA_BLOCK = 256
MOBA_TOPK = 3
Q_BLOCK = 128
N_BRANCH = 3
BRANCH_WIDTH = 8 * HEAD_DIM
D_FF = 4 * D_MODEL

LANES = 128
NEG_BIG = -1e30
Q_SCALE_LOG2 = (HEAD_DIM ** -0.5) * float(np.log2(np.e))
INT_MIN = -(2 ** 31)

COL_GATES = 0
COL_AQ = 3072
COL_BQ = 3584
COL_CQ = 4096
COL_CK = 4608
COL_IQ = 5120
COL_BK = 5376
COL_AKIK = 5504
ROPE_BEGIN, ROPE_END = 3072, 5632
COL_AVIW = 5632
COL_BV = 5760
COL_CV = 5888
N_PACKED = 6656

VMEM_LIMIT = 56 * 1024 * 1024


def _cparams(n_axes):
    return pltpu.CompilerParams(dimension_semantics=("arbitrary",) * n_axes,
                                vmem_limit_bytes=VMEM_LIMIT)


def _nt_dot(a, b):
    return lax.dot_general(a, b, (((1,), (1,)), ((), ())), preferred_element_type=F32)


def _split3_lhs(x):
    hi = x.astype(BF16)
    lo = (x - hi.astype(F32)).astype(BF16)
    return jnp.concatenate([hi, hi, lo], axis=1)


def _split3_rhs(x):
    hi = x.astype(BF16)
    lo = (x - hi.astype(F32)).astype(BF16)
    return jnp.concatenate([hi, lo, hi], axis=1)


def _proj_kernel(x_ref, g_ref, w_ref, cos_ref, sin_ref, o_ref, h_ref, *, rope_lo, rope_hi, tn):
    j = pl.program_id(1)

    @pl.when(j == 0)
    def _():
        x = x_ref[...]
        ms = jnp.mean(x * x, axis=-1, keepdims=True)
        h_ref[...] = (x * lax.rsqrt(ms + NORM_EPS) * g_ref[...]).astype(BF16)

    acc = jnp.dot(h_ref[...], w_ref[...], preferred_element_type=F32)
    is_rope = jnp.logical_and(j >= rope_lo, j < rope_hi)

    @pl.when(is_rope)
    def _():
        cos = cos_ref[...]
        sin = sin_ref[...]
        lane = lax.broadcasted_iota(I32, cos.shape, 1)
        first_half = (lane % HEAD_DIM) < (HEAD_DIM // 2)
        for c in range(tn // LANES):
            a = acc[:, c * LANES:(c + 1) * LANES]
            partner = jnp.where(first_half,
                                pltpu.roll(a, LANES - HEAD_DIM // 2, 1),
                                pltpu.roll(a, HEAD_DIM // 2, 1))
            o_ref[:, c * LANES:(c + 1) * LANES] = a * cos + partner * sin

    @pl.when(jnp.logical_not(is_rope))
    def _():
        o_ref[...] = acc


def _project(x2, gain, w_packed, cos_t, sin_t, seq, tm=512, tn=512):
    t = x2.shape[0]
    n = w_packed.shape[1]
    assert t % tm == 0 and seq % tm == 0 and n % tn == 0
    assert ROPE_BEGIN % tn == 0 and ROPE_END % tn == 0
    pos_blocks = seq // tm
    kern = functools.partial(_proj_kernel, rope_lo=ROPE_BEGIN // tn, rope_hi=ROPE_END // tn, tn=tn)
    return pl.pallas_call(
        kern,
        grid=(t // tm, n // tn),
        in_specs=[
            pl.BlockSpec((tm, D_MODEL), lambda i, j: (i, 0)),
            pl.BlockSpec((1, D_MODEL), lambda i, j: (0, 0)),
            pl.BlockSpec((D_MODEL, tn), lambda i, j: (0, j)),
            pl.BlockSpec((tm, LANES), lambda i, j: (i % pos_blocks, 0)),
            pl.BlockSpec((tm, LANES), lambda i, j: (i % pos_blocks, 0)),
        ],
        out_specs=pl.BlockSpec((tm, tn), lambda i, j: (i, j)),
        out_shape=jax.ShapeDtypeStruct((t, n), F32),
        scratch_shapes=[pltpu.VMEM((tm, D_MODEL), BF16)],
        compiler_params=_cparams(2),
        name="norm_proj_rope",
    )(x2, gain, w_packed, cos_t, sin_t)


def _value_with_ones(v):
    lane = lax.broadcasted_iota(I32, v.shape, 1)
    return jnp.concatenate([v.astype(BF16), jnp.where(lane == 0, 1.0, 0.0).astype(BF16)], axis=1)


def _dsa_kernel(aq_ref, iq_ref, kk_ref, vw_ref, o_ref,
                kbf_ref, vbf_ref, ikc_ref, keys_ref, qall_ref, mrow_ref, acc_ref, tri_ref, ones_ref,
                *, seq, kc, top_k):
    i = pl.program_id(1)
    qb = Q_BLOCK
    n_prep = seq // kc
    idx_scale = (IDX_DIM ** -0.5) * (IDX_HEADS ** -0.5)

    @pl.when(i == 0)
    def _prep():
        def body(c, carry):
            rows = pl.ds(pl.multiple_of(c * kc, kc), kc)
            kk = kk_ref[rows, :]
            kbf_ref[rows, :] = kk[:, :HEAD_DIM].astype(BF16)
            ikc_ref[rows, :] = _split3_rhs(kk[:, HEAD_DIM:])
            vbf_ref[rows, :] = _value_with_ones(vw_ref[rows, :HEAD_DIM])
            return carry
        lax.fori_loop(0, n_prep, body, 0)
        r = lax.broadcasted_iota(I32, (kc, kc), 0)
        cidx = lax.broadcasted_iota(I32, (kc, kc), 1)
        tri_ref[...] = jnp.where(r <= cidx, 1.0, 0.0).astype(BF16)
        ones_ref[...] = jnp.ones(ones_ref.shape, BF16)

    nkc = (i * qb) // kc + 1

    iq = iq_ref[...]
    w = vw_ref[pl.ds(pl.multiple_of(i * qb, qb), qb), HEAD_DIM:HEAD_DIM + IDX_HEADS] * idx_scale
    qcat = jnp.concatenate(
        [_split3_lhs(iq[:, h * IDX_DIM:(h + 1) * IDX_DIM]) for h in range(IDX_HEADS)], axis=0)
    row = lax.broadcasted_iota(I32, (qb, kc), 0)
    col = lax.broadcasted_iota(I32, (qb, kc), 1)
    qpos = i * qb + row

    def score_body(c, carry):
        rows = pl.ds(pl.multiple_of(c * kc, kc), kc)
        dots = _nt_dot(qcat, ikc_ref[rows, :])
        sc = jnp.zeros((qb, kc), F32)
        for h in range(IDX_HEADS):
            sc = sc + jnp.maximum(dots[h * qb:(h + 1) * qb], 0.0) * w[:, h:h + 1]
        sc = jnp.where(sc == 0.0, 0.0, sc)
        bits = lax.bitcast_convert_type(sc, I32)
        key = jnp.where(bits < 0, bits ^ jnp.int32(0x7FFFFFFF), bits)
        keys_ref[c] = jnp.where(c * kc + col <= qpos, key, jnp.int32(INT_MIN))
        return carry
    lax.fori_loop(0, nkc, score_body, 0)

    def count(pred, thr):
        thr_b = jnp.broadcast_to(thr, (qb, LANES))

        def body(c, acc):
            kch = keys_ref[c]
            for g in range(kc // LANES):
                acc = acc + jnp.where(pred(kch[:, g * LANES:(g + 1) * LANES], thr_b, c, g), 1.0, 0.0)
            return acc
        acc = lax.fori_loop(0, nkc, body, jnp.zeros((qb, LANES), F32))
        return jnp.sum(acc, axis=1, keepdims=True)

    ge = lambda k, t, c, g: k >= t
    kf = float(top_k)
    t0 = jnp.where(count(ge, jnp.zeros((qb, 1), I32)) >= kf, jnp.int32(0), jnp.int32(INT_MIN))

    def bit_body(p, t):
        cand = t | jnp.left_shift(jnp.int32(1), 30 - p)
        return jnp.where(count(ge, cand) >= kf, cand, t)
    thr = lax.fori_loop(0, 31, bit_body, t0)

    n_gt = count(lambda k, t, c, g: k > t, thr)
    need = jnp.where(thr > jnp.int32(INT_MIN), kf - n_gt, 0.0)

    aq = aq_ref[...]
    for h in range(A_HEADS):
        qall_ref[h * qb:(h + 1) * qb, :] = (
            aq[:, h * HEAD_DIM:(h + 1) * HEAD_DIM] * Q_SCALE_LOG2).astype(BF16)
    thr_b = jnp.broadcast_to(thr, (qb, LANES))
    need_b = jnp.broadcast_to(need, (qb, LANES))
    ngrp = kc // LANES

    def masked_logits(c, bias):
        rows = pl.ds(pl.multiple_of(c * kc, kc), kc)
        s = _nt_dot(qall_ref[...], kbf_ref[rows, :])
        return (s.reshape(A_HEADS, qb, kc) + bias[None]).reshape(A_HEADS * qb, kc)

    def max_body(c, ties_before):
        kch = keys_ref[c]
        eq = [kch[:, g * LANES:(g + 1) * LANES] == thr_b for g in range(ngrp)]
        eq_b = jnp.concatenate([jnp.where(e, 1.0, 0.0).astype(BF16) for e in eq], axis=1)
        rank = jnp.dot(eq_b, tri_ref[...], preferred_element_type=F32)
        bias = jnp.concatenate(
            [jnp.where(jnp.logical_or(
                kch[:, g * LANES:(g + 1) * LANES] > thr_b,
                jnp.logical_and(eq[g], rank[:, g * LANES:(g + 1) * LANES] + ties_before <= need_b)),
                0.0, NEG_BIG) for g in range(ngrp)], axis=1)
        keys_ref[c] = lax.bitcast_convert_type(bias, I32)
        s = masked_logits(c, bias)
        m = mrow_ref[...]
        for g in range(ngrp):
            m = jnp.maximum(m, s[:, g * LANES:(g + 1) * LANES])
        mrow_ref[...] = m
        return ties_before + jnp.dot(eq_b, ones_ref[...], preferred_element_type=F32)
    mrow_ref[...] = jnp.full(mrow_ref.shape, NEG_BIG, F32)
    lax.fori_loop(0, nkc, max_body, jnp.zeros((qb, LANES), F32))
    mrow_ref[...] = jnp.broadcast_to(jnp.max(mrow_ref[...], axis=1, keepdims=True), mrow_ref.shape)

    def pv_body(c, carry):
        rows = pl.ds(pl.multiple_of(c * kc, kc), kc)
        s = masked_logits(c, lax.bitcast_convert_type(keys_ref[c], F32))
        m = mrow_ref[...]
        p = jnp.concatenate(
            [jnp.exp2(s[:, g * LANES:(g + 1) * LANES] - m).astype(BF16) for g in range(ngrp)], axis=1)
        acc_ref[...] += jnp.dot(p, vbf_ref[rows, :], preferred_element_type=F32)
        return carry
    acc_ref[...] = jnp.zeros(acc_ref.shape, F32)
    lax.fori_loop(0, nkc, pv_body, 0)

    acc = acc_ref[...]
    out = acc[:, :HEAD_DIM] / acc[:, HEAD_DIM:HEAD_DIM + 1]
    for h in range(A_HEADS):
        o_ref[:, h * HEAD_DIM:(h + 1) * HEAD_DIM] = out[h * qb:(h + 1) * qb].astype(o_ref.dtype)


def _dsa(proj3, kc=512):
    b, seq, _ = proj3.shape
    top_k = min(IDX_TOPK_MAX, seq // 4)
    kc = min(kc, seq)
    assert seq % kc == 0 and kc % Q_BLOCK == 0 and kc >= top_k
    kern = functools.partial(_dsa_kernel, seq=seq, kc=kc, top_k=top_k)
    hq = A_HEADS * Q_BLOCK
    return pl.pallas_call(
        kern,
        grid=(b, seq // Q_BLOCK),
        in_specs=[
            pl.BlockSpec((None, Q_BLOCK, A_HEADS * HEAD_DIM), lambda bb, i: (bb, i, COL_AQ // 512)),
            pl.BlockSpec((None, Q_BLOCK, IDX_HEADS * IDX_DIM), lambda bb, i: (bb, i, COL_IQ // 256)),
            pl.BlockSpec((None, seq, LANES), lambda bb, i: (bb, 0, COL_AKIK // LANES)),
            pl.BlockSpec((None, seq, LANES), lambda bb, i: (bb, 0, COL_AVIW // LANES)),
        ],
        out_specs=pl.BlockSpec((None, Q_BLOCK, A_HEADS * HEAD_DIM), lambda bb, i: (bb, i, 0)),
        out_shape=jax.ShapeDtypeStruct((b, seq, A_HEADS * HEAD_DIM), BF16),
        scratch_shapes=[
            pltpu.VMEM((seq, HEAD_DIM), BF16),
            pltpu.VMEM((seq, LANES), BF16),
            pltpu.VMEM((seq, 3 * IDX_DIM), BF16),
            pltpu.VMEM((seq // kc, Q_BLOCK, kc), I32),
            pltpu.VMEM((hq, HEAD_DIM), BF16),
            pltpu.VMEM((hq, LANES), F32),
            pltpu.VMEM((hq, LANES), F32),
            pltpu.VMEM((kc, kc), BF16),
            pltpu.VMEM((kc, LANES), BF16),
        ],
        compiler_params=_cparams(2),
        name="dsa_topk_attention",
    )(proj3, proj3, proj3, proj3)


def _swa_kernel(sink_ref, q_ref, kp_ref, kc_ref, vp_ref, vc_ref, o_ref):
    i = pl.program_id(1)
    w = WINDOW
    group = B_HEADS // B_KV_HEADS
    q = q_ref[...]
    k = jnp.concatenate([kp_ref[...], kc_ref[...]], axis=0)
    v = jnp.concatenate([vp_ref[...], vc_ref[...]], axis=0)
    row = lax.broadcasted_iota(I32, (w, 2 * w), 0)
    col = lax.broadcasted_iota(I32, (w, 2 * w), 1)
    kpos = (i - 1) * w + col
    diff = i * w + row - kpos
    mask = jnp.logical_and(jnp.logical_and(diff >= 0, diff < w), kpos >= 0)
    for g in range(B_KV_HEADS):
        kg = k[:, g * HEAD_DIM:(g + 1) * HEAD_DIM].astype(BF16)
        vg = v[:, g * HEAD_DIM:(g + 1) * HEAD_DIM].astype(BF16)
        for hh in range(group):
            h = g * group + hh
            qh = (q[:, h * HEAD_DIM:(h + 1) * HEAD_DIM] * (HEAD_DIM ** -0.5)).astype(BF16)
            s = jnp.where(mask, _nt_dot(qh, kg), NEG_BIG)
            sink = sink_ref[h]
            m = jnp.maximum(jnp.max(s, axis=1, keepdims=True), sink)
            p = jnp.exp(s - m)
            den = jnp.sum(p, axis=1, keepdims=True) + jnp.exp(sink - m)
            o = jnp.dot(p.astype(BF16), vg, preferred_element_type=F32) / den
            o_ref[:, h * HEAD_DIM:(h + 1) * HEAD_DIM] = o.astype(o_ref.dtype)


def _swa(proj3, sinks):
    b, seq, _ = proj3.shape
    w = WINDOW
    kvw = B_KV_HEADS * HEAD_DIM
    assert kvw == LANES and seq % w == 0
    prev = lambda col: (lambda bb, i: (bb, jnp.maximum(i - 1, 0), col))
    cur = lambda col: (lambda bb, i: (bb, i, col))
    return pl.pallas_call(
        _swa_kernel,
        grid=(b, seq // w),
        in_specs=[
            pl.BlockSpec(memory_space=pltpu.SMEM),
            pl.BlockSpec((None, w, B_HEADS * HEAD_DIM), cur(COL_BQ // 512)),
            pl.BlockSpec((None, w, kvw), prev(COL_BK // LANES)),
            pl.BlockSpec((None, w, kvw), cur(COL_BK // LANES)),
            pl.BlockSpec((None, w, kvw), prev(COL_BV // LANES)),
            pl.BlockSpec((None, w, kvw), cur(COL_BV // LANES)),
        ],
        out_specs=pl.BlockSpec((None, w, B_HEADS * HEAD_DIM), lambda bb, i: (bb, i, 0)),
        out_shape=jax.ShapeDtypeStruct((b, seq, B_HEADS * HEAD_DIM), BF16),
        compiler_params=_cparams(2),
        name="swa_sink_attention",
    )(sinks, proj3, proj3, proj3, proj3, proj3)


def _moba_kernel(q_ref, k_ref, v_ref, o_ref,
                 kaug_ref, vaug_ref, kmean_ref, kmw_ref, qaug_ref, mrow_ref, acc_ref,
                 *, seq, tq, n_sel):
    i = pl.program_id(2)
    blk = MOBA_BLOCK
    nkb = seq // blk
    heads = LANES // HEAD_DIM
    assert heads == 2
    ones_lane = (HEAD_DIM, 0)

    @pl.when(i == 0)
    def _prep():
        kmean_ref[...] = jnp.zeros(kmean_ref.shape, F32)
        lane_b = lax.broadcasted_iota(I32, (blk, LANES), 1)

        def body(n, carry):
            rows = pl.ds(pl.multiple_of(n * blk, blk), blk)
            kb = k_ref[rows, :]
            vb = v_ref[rows, :]
            onehot = jnp.where(lane_b == n, 1.0, 0.0).astype(BF16)
            for e in range(heads):
                mine = (lane_b // HEAD_DIM) == e
                kaug_ref[e, rows, :] = jnp.concatenate(
                    [onehot, jnp.where(mine, kb, 0.0).astype(BF16)], axis=1)
                ones = jnp.where(lane_b == ones_lane[e], 1.0, 0.0)
                vaug_ref[e, rows, :] = jnp.where(mine, vb, ones).astype(BF16)
            kmean_ref[pl.ds(n, 1), :] = jnp.sum(kb, axis=0, keepdims=True) * (1.0 / blk)
            return carry
        lax.fori_loop(0, nkb, body, 0)
        km = kmean_ref[...]
        lane_m = lax.broadcasted_iota(I32, km.shape, 1)
        kmw = jnp.concatenate(
            [jnp.where((lane_m // HEAD_DIM) == e, km, 0.0) for e in range(heads)], axis=0)
        hi = kmw.astype(BF16)
        kmw_ref[0] = hi
        kmw_ref[1] = (kmw - hi.astype(F32)).astype(BF16)

    q = q_ref[...]
    own1 = (i * tq + lax.broadcasted_iota(I32, (tq, 1), 0)) // blk
    lane = lax.broadcasted_iota(I32, (tq, LANES), 1)
    lanef = lane.astype(F32)

    qhi = q.astype(BF16)
    qlo = (q - qhi.astype(F32)).astype(BF16)
    gates = (_nt_dot(qhi, kmw_ref[0]) + _nt_dot(qhi, kmw_ref[1])) + _nt_dot(qlo, kmw_ref[0])
    q2 = (q * Q_SCALE_LOG2).astype(BF16)
    for e in range(heads):
        g = jnp.where(lane < own1, gates[:, e * LANES:(e + 1) * LANES], -jnp.inf)
        allowed = lane == own1
        for _ in range(n_sel):
            gmax = jnp.max(g, axis=1, keepdims=True)
            is_max = jnp.logical_and(g == gmax, gmax > -jnp.inf)
            first = jnp.min(jnp.where(is_max, lanef, float(LANES)), axis=1, keepdims=True)
            onehot = lanef == first
            allowed = jnp.logical_or(allowed, onehot)
            g = jnp.where(onehot, -jnp.inf, g)
        bias = jnp.where(allowed, 0.0, NEG_BIG).astype(BF16)
        qaug_ref[e] = jnp.concatenate([bias, q2], axis=1)

    kc = tq
    ngrp = kc // LANES
    col = lax.broadcasted_iota(I32, (tq, kc), 1)
    row = lax.broadcasted_iota(I32, (tq, kc), 0)

    def logits(e, c, diagonal):
        rows = pl.ds(pl.multiple_of(c * kc, kc), kc)
        s = _nt_dot(qaug_ref[e], kaug_ref[e, rows, :])
        return jnp.where(col <= row, s, NEG_BIG) if diagonal else s

    def update_max(e, c, diagonal):
        s = logits(e, c, diagonal)
        m = mrow_ref[e]
        for gi in range(ngrp):
            m = jnp.maximum(m, s[:, gi * LANES:(gi + 1) * LANES])
        mrow_ref[e] = m

    def max_body(c, carry):
        for e in range(heads):
            update_max(e, c, False)
        return carry
    mrow_ref[...] = jnp.full(mrow_ref.shape, NEG_BIG, F32)
    lax.fori_loop(0, i, max_body, 0)
    for e in range(heads):
        update_max(e, i, True)
        mrow_ref[e] = jnp.broadcast_to(jnp.max(mrow_ref[e], axis=1, keepdims=True), (tq, LANES))

    def update_acc(e, c, diagonal):
        rows = pl.ds(pl.multiple_of(c * kc, kc), kc)
        s = logits(e, c, diagonal)
        m = mrow_ref[e]
        p = jnp.concatenate(
            [jnp.exp2(s[:, gi * LANES:(gi + 1) * LANES] - m).astype(BF16) for gi in range(ngrp)], axis=1)
        acc_ref[e] += jnp.dot(p, vaug_ref[e, rows, :], preferred_element_type=F32)

    def pv_body(c, carry):
        for e in range(heads):
            update_acc(e, c, False)
        return carry
    acc_ref[...] = jnp.zeros(acc_ref.shape, F32)
    lax.fori_loop(0, i, pv_body, 0)
    for e in range(heads):
        update_acc(e, i, True)

    outs = []
    for e in range(heads):
        acc = acc_ref[e]
        outs.append(acc / acc[:, ones_lane[e]:ones_lane[e] + 1])
    o_ref[...] = jnp.where(lane < HEAD_DIM, outs[0], outs[1]).astype(o_ref.dtype)


def _moba(proj3, tq=512):
    b, seq, _ = proj3.shape
    tq = min(tq, seq)
    blk = MOBA_BLOCK
    assert seq % blk == 0 and tq % blk == 0 and seq % tq == 0
    nkb = seq // blk
    assert nkb <= LANES
    n_sel = min(MOBA_TOPK, nkb - 1)
    heads = LANES // HEAD_DIM
    kern = functools.partial(_moba_kernel, seq=seq, tq=tq, n_sel=n_sel)
    return pl.pallas_call(
        kern,
        grid=(b, C_HEADS // heads, seq // tq),
        in_specs=[
            pl.BlockSpec((None, tq, LANES), lambda bb, hp, i: (bb, i, COL_CQ // LANES + hp)),
            pl.BlockSpec((None, seq, LANES), lambda bb, hp, i: (bb, 0, COL_CK // LANES + hp)),
            pl.BlockSpec((None, seq, LANES), lambda bb, hp, i: (bb, 0, COL_CV // LANES + hp)),
        ],
        out_specs=pl.BlockSpec((None, tq, LANES), lambda bb, hp, i: (bb, i, hp)),
        out_shape=jax.ShapeDtypeStruct((b, seq, C_HEADS * HEAD_DIM), BF16),
        scratch_shapes=[
            pltpu.VMEM((heads, seq, 2 * LANES), BF16),
            pltpu.VMEM((heads, seq, LANES), BF16),
            pltpu.VMEM((LANES, LANES), F32),
            pltpu.VMEM((2, heads * LANES, LANES), BF16),
            pltpu.VMEM((heads, tq, 2 * LANES), BF16),
            pltpu.VMEM((heads, tq, LANES), F32),
            pltpu.VMEM((heads, tq, LANES), F32),
        ],
        compiler_params=_cparams(3),
        name="moba_attention",
    )(proj3, proj3, proj3)


def _merge_kernel(x_ref, oa_ref, ob_ref, oc_ref, g_ref, wb_ref, wo_ref, o_ref):
    merged = None
    for n, br in enumerate((oa_ref, ob_ref, oc_ref)):
        y = jnp.dot(br[...], wb_ref[n], preferred_element_type=F32)
        term = jax.nn.sigmoid(g_ref[:, n * D_MODEL:(n + 1) * D_MODEL]) * y
        merged = term if merged is None else merged + term
    o_ref[...] = x_ref[...] + jnp.dot(merged.astype(BF16), wo_ref[...], preferred_element_type=F32)


def _merge(x2, oa, ob, oc, proj2, w_branch, w_out, tm=512):
    t = x2.shape[0]
    assert t % tm == 0
    row = lambda i: (i, 0)
    return pl.pallas_call(
        _merge_kernel,
        grid=(t // tm,),
        in_specs=[
            pl.BlockSpec((tm, D_MODEL), row),
            pl.BlockSpec((tm, BRANCH_WIDTH), row),
            pl.BlockSpec((tm, BRANCH_WIDTH), row),
            pl.BlockSpec((tm, BRANCH_WIDTH), row),
            pl.BlockSpec((tm, N_BRANCH * D_MODEL), lambda i: (i, COL_GATES // (N_BRANCH * D_MODEL))),
            pl.BlockSpec((N_BRANCH, BRANCH_WIDTH, D_MODEL), lambda i: (0, 0, 0)),
            pl.BlockSpec((D_MODEL, D_MODEL), lambda i: (0, 0)),
        ],
        out_specs=pl.BlockSpec((tm, D_MODEL), row),
        out_shape=jax.ShapeDtypeStruct((t, D_MODEL), F32),
        compiler_params=_cparams(1),
        name="gated_merge_out_proj",
    )(x2, oa, ob, oc, proj2, w_branch, w_out)


def _ffn_kernel(x_ref, g_ref, w1_ref, w2_ref, gf_ref, o_ref, h_ref, acc_ref, *, final_norm):
    j = pl.program_id(1)

    @pl.when(j == 0)
    def _():
        x = x_ref[...]
        ms = jnp.mean(x * x, axis=-1, keepdims=True)
        h_ref[...] = (x * lax.rsqrt(ms + NORM_EPS) * g_ref[...]).astype(BF16)
        acc_ref[...] = jnp.zeros(acc_ref.shape, F32)

    u = jnp.maximum(jnp.dot(h_ref[...], w1_ref[...], preferred_element_type=F32), 0.0)
    acc_ref[...] += jnp.dot((u * u).astype(BF16), w2_ref[...], preferred_element_type=F32)

    @pl.when(j == pl.num_programs(1) - 1)
    def _():
        y = x_ref[...] + acc_ref[...]
        if final_norm:
            ms = jnp.mean(y * y, axis=-1, keepdims=True)
            y = y * lax.rsqrt(ms + NORM_EPS) * gf_ref[...]
        o_ref[...] = y


def _ffn(x2, gain, w1, w2, gain_f, final_norm, tm=512, tf=1024):
    t = x2.shape[0]
    assert t % tm == 0 and D_FF % tf == 0
    kern = functools.partial(_ffn_kernel, final_norm=final_norm)
    return pl.pallas_call(
        kern,
        grid=(t // tm, D_FF // tf),
        in_specs=[
            pl.BlockSpec((tm, D_MODEL), lambda i, j: (i, 0)),
            pl.BlockSpec((1, D_MODEL), lambda i, j: (0, 0)),
            pl.BlockSpec((D_MODEL, tf), lambda i, j: (0, j)),
            pl.BlockSpec((tf, D_MODEL), lambda i, j: (j, 0)),
            pl.BlockSpec((1, D_MODEL), lambda i, j: (0, 0)),
        ],
        out_specs=pl.BlockSpec((tm, D_MODEL), lambda i, j: (i, 0)),
        out_shape=jax.ShapeDtypeStruct((t, D_MODEL), F32),
        scratch_shapes=[pltpu.VMEM((tm, D_MODEL), BF16), pltpu.VMEM((tm, D_MODEL), F32)],
        compiler_params=_cparams(2),
        name="relu2_mlp",
    )(x2, gain, w1, w2, gain_f)


def _pack_w_in(w_in):
    sizes = (A_HEADS * HEAD_DIM, HEAD_DIM, HEAD_DIM, IDX_HEADS * IDX_DIM, IDX_DIM, IDX_HEADS,
             B_HEADS * HEAD_DIM, B_KV_HEADS * HEAD_DIM, B_KV_HEADS * HEAD_DIM,
             C_HEADS * HEAD_DIM, C_HEADS * HEAD_DIM, C_HEADS * HEAD_DIM, N_BRANCH * D_MODEL)
    offs = np.concatenate([[0], np.cumsum(sizes)])
    aq, ak, av, iq, ik, iw, bq, bk, bv, cq, ck, cv, gates = [
        w_in[..., offs[n]:offs[n + 1]] for n in range(len(sizes))]
    lead = w_in.shape[:-1]
    pad_iw = jnp.zeros(lead + (LANES - HEAD_DIM - IDX_HEADS,), w_in.dtype)
    parts = [gates, aq, bq, cq, ck, iq, bk, ak, ik, av, iw, pad_iw, bv, cv]
    used = sum(p.shape[-1] for p in parts)
    parts.append(jnp.zeros(lead + (N_PACKED - used,), w_in.dtype))
    return jnp.concatenate(parts, axis=-1).astype(BF16)


def _rope_tables(seq):
    half = HEAD_DIM // 2
    inv = 1.0 / (ROPE_THETA ** (jnp.arange(0, HEAD_DIM, 2, dtype=F32) / HEAD_DIM))
    ang = jnp.arange(seq, dtype=F32)[:, None] * inv[None, :]
    cos, sin = jnp.cos(ang), jnp.sin(ang)
    reps = LANES // HEAD_DIM
    cos_t = jnp.tile(jnp.concatenate([cos, cos], axis=1), (1, reps))
    sin_t = jnp.tile(jnp.concatenate([-sin, sin], axis=1), (1, reps))
    assert cos_t.shape == (seq, LANES) and half * 2 == HEAD_DIM
    return cos_t, sin_t


def kernel(x, norm1, w_in, sinks, w_branch, w_out, norm2, w_ff1, w_ff2, norm_f):
    b, seq, d = x.shape
    depth = w_in.shape[0]
    cos_t, sin_t = _rope_tables(seq)
    w_in_p = _pack_w_in(w_in)
    w_branch_b = w_branch.astype(BF16)
    w_out_b = w_out.astype(BF16)
    w_ff1_b = w_ff1.astype(BF16)
    w_ff2_b = w_ff2.astype(BF16)
    gain_f = norm_f.reshape(1, d)

    x2 = x.reshape(b * seq, d)
    for l in range(depth):
        proj2 = _project(x2, norm1[l].reshape(1, d), w_in_p[l], cos_t, sin_t, seq)
        proj3 = proj2.reshape(b, seq, N_PACKED)
        o_a = _dsa(proj3).reshape(b * seq, BRANCH_WIDTH)
        o_b = _swa(proj3, sinks[l]).reshape(b * seq, BRANCH_WIDTH)
        o_c = _moba(proj3).reshape(b * seq, BRANCH_WIDTH)
        x2 = _merge(x2, o_a, o_b, o_c, proj2, w_branch_b[l], w_out_b[l])
        x2 = _ffn(x2, norm2[l].reshape(1, d), w_ff1_b[l], w_ff2_b[l], gain_f,
                  final_norm=(l == depth - 1))
    return x2.reshape(b, seq, d)
```

```python
import functools

import jax
import jax.numpy as jnp
import numpy as np
from jax import lax
from jax.experimental import pallas as pl
from jax.experimental.pallas import tpu as pltpu

F32 = jnp.float32
BF16 = jnp.bfloat16
I32 = jnp.int32

D_MODEL = 1024
HEAD_DIM = 64
ROPE_THETA = 10000.0
NORM_EPS = 1e-6
A_HEADS = 8
IDX_HEADS = 4
IDX_DIM = 64
IDX_TOPK_MAX = 256
B_HEADS = 8
B_KV_HEADS = 2
WINDOW = 128
C_HEADS = 8
MOBA_BLOCK = 256
MOBA_TOPK = 3
Q_BLOCK = 128
N_BRANCH = 3
BRANCH_WIDTH = 8 * HEAD_DIM
D_FF = 4 * D_MODEL

LANES = 128
NEG_BIG = -1e30
Q_SCALE_LOG2 = (HEAD_DIM ** -0.5) * float(np.log2(np.e))
INT_MIN = -(2 ** 31)

COL_GATES = 0
COL_AQ = 3072
COL_BQ = 3584
COL_CQ = 4096
COL_CK = 4608
COL_IQ = 5120
COL_BK = 5376
COL_AKIK = 5504
ROPE_BEGIN, ROPE_END = 3072, 5632
COL_AVIW = 5632
COL_BV = 5760
COL_CV = 5888
N_PACKED = 6656

VMEM_LIMIT = 56 * 1024 * 1024


def _cparams(n_axes):
    return pltpu.CompilerParams(dimension_semantics=("arbitrary",) * n_axes,
                                vmem_limit_bytes=VMEM_LIMIT)


def _nt_dot(a, b):
    return lax.dot_general(a, b, (((1,), (1,)), ((), ())), preferred_element_type=F32)


def _pipelined(n, produce, consume, init):
    produce(0, 0)

    def pair(t, carry):
        c = 2 * t
        produce(c + 1, 1)
        carry = consume(c, 0, carry, False)
        produce(c + 2, 0)
        return consume(c + 1, 1, carry, False)
    n_pairs = (n - 1) // 2
    carry = lax.fori_loop(0, n_pairs, pair, init)
    c0 = 2 * n_pairs

    def two_left(carry):
        produce(c0 + 1, 1)
        carry = consume(c0, 0, carry, False)
        return consume(c0 + 1, 1, carry, True)

    def one_left(carry):
        return consume(c0, 0, carry, True)
    return lax.cond(n - c0 == 2, two_left, one_left, carry)


def _split3_lhs(x):
    hi = x.astype(BF16)
    lo = (x - hi.astype(F32)).astype(BF16)
    return jnp.concatenate([hi, hi, lo], axis=1)


def _split3_rhs(x):
    hi = x.astype(BF16)
    lo = (x - hi.astype(F32)).astype(BF16)
    return jnp.concatenate([hi, lo, hi], axis=1)


def _proj_kernel(x_ref, g_ref, w_ref, cos_ref, sin_ref, o_ref, h_ref, *, rope_lo, rope_hi, tn):
    j = pl.program_id(1)

    @pl.when(j == 0)
    def _():
        x = x_ref[...]
        ms = jnp.mean(x * x, axis=-1, keepdims=True)
        h_ref[...] = (x * lax.rsqrt(ms + NORM_EPS) * g_ref[...]).astype(BF16)

    acc = jnp.dot(h_ref[...], w_ref[...], preferred_element_type=F32)
    is_rope = jnp.logical_and(j >= rope_lo, j < rope_hi)

    @pl.when(is_rope)
    def _():
        cos = cos_ref[...]
        sin = sin_ref[...]
        lane = lax.broadcasted_iota(I32, cos.shape, 1)
        first_half = (lane % HEAD_DIM) < (HEAD_DIM // 2)
        for c in range(tn // LANES):
            a = acc[:, c * LANES:(c + 1) * LANES]
            partner = jnp.where(first_half,
                                pltpu.roll(a, LANES - HEAD_DIM // 2, 1),
                                pltpu.roll(a, HEAD_DIM // 2, 1))
            o_ref[:, c * LANES:(c + 1) * LANES] = a * cos + partner * sin

    @pl.when(jnp.logical_not(is_rope))
    def _():
        o_ref[...] = acc


def _project(x2, gain, w_packed, cos_t, sin_t, seq, tm=1024, tn=512):
    t = x2.shape[0]
    tm = min(tm, seq)
    n = w_packed.shape[1]
    assert t % tm == 0 and seq % tm == 0 and n % tn == 0
    assert ROPE_BEGIN % tn == 0 and ROPE_END % tn == 0
    pos_blocks = seq // tm
    kern = functools.partial(_proj_kernel, rope_lo=ROPE_BEGIN // tn, rope_hi=ROPE_END // tn, tn=tn)
    return pl.pallas_call(
        kern,
        grid=(t // tm, n // tn),
        in_specs=[
            pl.BlockSpec((tm, D_MODEL), lambda i, j: (i, 0)),
            pl.BlockSpec((1, D_MODEL), lambda i, j: (0, 0)),
            pl.BlockSpec((D_MODEL, tn), lambda i, j: (0, j)),
            pl.BlockSpec((tm, LANES), lambda i, j: (i % pos_blocks, 0)),
            pl.BlockSpec((tm, LANES), lambda i, j: (i % pos_blocks, 0)),
        ],
        out_specs=pl.BlockSpec((tm, tn), lambda i, j: (i, j)),
        out_shape=jax.ShapeDtypeStruct((t, n), F32),
        scratch_shapes=[pltpu.VMEM((tm, D_MODEL), BF16)],
        compiler_params=_cparams(2),
        name="norm_proj_rope",
    )(x2, gain, w_packed, cos_t, sin_t)


def _value_with_ones(v):
    lane = lax.broadcasted_iota(I32, v.shape, 1)
    return jnp.concatenate([v.astype(BF16), jnp.where(lane == 0, 1.0, 0.0).astype(BF16)], axis=1)


def _dsa_kernel(aq_ref, iq_ref, kk_ref, vw_ref, o_ref,
                kbf_ref, vbf_ref, ikc_ref, keys_ref, qall_ref, mrow_ref, acc_ref, tri_ref, ones_ref,
                sbuf_ref, *, seq, kc, top_k):
    i = pl.program_id(1)
    qb = Q_BLOCK
    n_prep = seq // kc
    idx_scale = (IDX_DIM ** -0.5) * (IDX_HEADS ** -0.5)

    @pl.when(i == 0)
    def _prep():
        def body(c, carry):
            rows = pl.ds(pl.multiple_of(c * kc, kc), kc)
            kk = kk_ref[rows, :]
            kbf_ref[rows, :] = kk[:, :HEAD_DIM].astype(BF16)
            ikc_ref[rows, :] = _split3_rhs(kk[:, HEAD_DIM:])
            vbf_ref[rows, :] = _value_with_ones(vw_ref[rows, :HEAD_DIM])
            return carry
        lax.fori_loop(0, n_prep, body, 0)
        r = lax.broadcasted_iota(I32, (kc, kc), 0)
        cidx = lax.broadcasted_iota(I32, (kc, kc), 1)
        tri_ref[...] = jnp.where(r <= cidx, 1.0, 0.0).astype(BF16)
        ones_ref[...] = jnp.ones(ones_ref.shape, BF16)

    nkc = (i * qb) // kc + 1

    iq = iq_ref[...]
    w = vw_ref[pl.ds(pl.multiple_of(i * qb, qb), qb), HEAD_DIM:HEAD_DIM + IDX_HEADS] * idx_scale
    qcat = jnp.concatenate(
        [_split3_lhs(iq[:, h * IDX_DIM:(h + 1) * IDX_DIM]) for h in range(IDX_HEADS)], axis=0)
    row = lax.broadcasted_iota(I32, (qb, LANES), 0)
    lane = lax.broadcasted_iota(I32, (qb, LANES), 1)
    qpos = i * qb + row
    w_b = [jnp.broadcast_to(w[:, h:h + 1], (qb, LANES)) for h in range(IDX_HEADS)]
    ngrp = kc // LANES

    def chunk_rows(c):
        return pl.ds(pl.multiple_of(c * kc, kc), kc)

    def to_key(x):
        bits = lax.bitcast_convert_type(x, I32)
        return jnp.where(bits < 0, bits ^ jnp.int32(0x7FFFFFFF), bits)

    def idx_dots(c, slot):
        sbuf_ref[slot, :IDX_HEADS * qb] = _nt_dot(qcat, ikc_ref[chunk_rows(c), :])

    def score_chunk(c, slot, smax, is_last):
        for g in range(ngrp):
            sc = None
            for h in range(IDX_HEADS):
                d = sbuf_ref[slot, h * qb:(h + 1) * qb, g * LANES:(g + 1) * LANES]
                term = jnp.maximum(d, 0.0) * w_b[h]
                sc = term if sc is None else sc + term
            sc = jnp.where(sc == 0.0, 0.0, sc)
            causal = c * kc + g * LANES + lane <= qpos
            keys_ref[c, :, g * LANES:(g + 1) * LANES] = jnp.where(causal, to_key(sc), jnp.int32(INT_MIN))
            smax = jnp.maximum(smax, jnp.where(causal, sc, -jnp.inf))
        return smax

    smax = _pipelined(nkc, idx_dots, score_chunk, jnp.full((qb, LANES), -jnp.inf, F32))
    key_max = to_key(jnp.max(smax, axis=1, keepdims=True))

    def count(pred, thr):
        thr_b = jnp.broadcast_to(thr, (qb, LANES))

        def body(c, acc):
            kch = keys_ref[c]
            for g in range(kc // LANES):
                acc = acc + jnp.where(pred(kch[:, g * LANES:(g + 1) * LANES], thr_b), 1.0, 0.0)
            return acc
        acc = lax.fori_loop(0, nkc, body, jnp.zeros((qb, LANES), F32))
        return jnp.sum(acc, axis=1, keepdims=True)

    ge = lambda k, t: k >= t
    kf = float(top_k)

    def probe(v, state):
        lo, n_lo, hi = state
        n_v = count(ge, v)
        inside = jnp.logical_and(v > lo, v < hi)
        up = jnp.logical_and(inside, n_v >= kf)
        down = jnp.logical_and(inside, n_v < kf)
        return jnp.where(up, v, lo), jnp.where(up, n_v, n_lo), jnp.where(down, v, hi)

    def unfinished(state):
        lo, n_lo, hi = state
        done = jnp.logical_or(n_lo == kf, hi - 1 == lo)
        return jnp.max(jnp.where(done, 0.0, 1.0)) > 0.0

    int_min = jnp.full((qb, 1), INT_MIN, I32)
    state = (int_min, jnp.full((qb, 1), float(seq + 1), F32), key_max + 1)
    state = probe(jnp.ones((qb, 1), I32), state)
    state = probe(jnp.zeros((qb, 1), I32), state)
    two_binades = jnp.int32(1 << 24)
    state = probe(jnp.maximum(key_max, jnp.int32(INT_MIN) + two_binades) - two_binades, state)

    def bisect_cond(carry):
        return jnp.logical_and(carry[0], carry[1] < 34)

    def bisect_body(carry):
        _, n_iter, lo, n_lo, hi = carry
        mid = lo + lax.shift_right_logical(hi - lo, jnp.int32(1))
        state = probe(mid, (lo, n_lo, hi))
        return (unfinished(state), n_iter + 1) + state
    carry = lax.while_loop(bisect_cond, bisect_body, (unfinished(state), jnp.int32(0)) + state)
    thr = carry[2]

    n_gt = count(lambda k, t: k > t, thr)
    need = jnp.where(thr > jnp.int32(INT_MIN), kf - n_gt, 0.0)

    aq = aq_ref[...]
    for h in range(A_HEADS):
        qall_ref[h * qb:(h + 1) * qb, :] = (
            aq[:, h * HEAD_DIM:(h + 1) * HEAD_DIM] * Q_SCALE_LOG2).astype(BF16)
    thr_b = jnp.broadcast_to(thr, (qb, LANES))
    need_b = jnp.broadcast_to(need, (qb, LANES))

    def logits(c, slot):
        sbuf_ref[slot] = _nt_dot(qall_ref[...], kbf_ref[chunk_rows(c), :])

    def masked(slot, bias):
        return (sbuf_ref[slot].reshape(A_HEADS, qb, kc) + bias[None]).reshape(A_HEADS * qb, kc)

    def max_chunk(c, slot, ties_before, is_last):
        kch = keys_ref[c]
        eq = [kch[:, g * LANES:(g + 1) * LANES] == thr_b for g in range(ngrp)]
        eq_b = jnp.concatenate([jnp.where(e, 1.0, 0.0).astype(BF16) for e in eq], axis=1)
        rank = jnp.dot(eq_b, tri_ref[...], preferred_element_type=F32)
        bias = jnp.concatenate(
            [jnp.where(jnp.logical_or(
                kch[:, g * LANES:(g + 1) * LANES] > thr_b,
                jnp.logical_and(eq[g], rank[:, g * LANES:(g + 1) * LANES] + ties_before <= need_b)),
                0.0, NEG_BIG) for g in range(ngrp)], axis=1)
        keys_ref[c] = lax.bitcast_convert_type(bias, I32)
        s = masked(slot, bias)
        m = mrow_ref[...]
        for g in range(ngrp):
            m = jnp.maximum(m, s[:, g * LANES:(g + 1) * LANES])
        mrow_ref[...] = m
        return ties_before + jnp.dot(eq_b, ones_ref[...], preferred_element_type=F32)
    mrow_ref[...] = jnp.full(mrow_ref.shape, NEG_BIG, F32)
    _pipelined(nkc, logits, max_chunk, jnp.zeros((qb, LANES), F32))
    mrow_ref[...] = jnp.broadcast_to(jnp.max(mrow_ref[...], axis=1, keepdims=True), mrow_ref.shape)

    def pv_chunk(c, slot, carry, is_last):
        s = masked(slot, lax.bitcast_convert_type(keys_ref[c], F32))
        m = mrow_ref[...]
        p = jnp.concatenate(
            [jnp.exp2(s[:, g * LANES:(g + 1) * LANES] - m).astype(BF16) for g in range(ngrp)], axis=1)
        acc_ref[...] += jnp.dot(p, vbf_ref[chunk_rows(c), :], preferred_element_type=F32)
        return carry
    acc_ref[...] = jnp.zeros(acc_ref.shape, F32)
    _pipelined(nkc, logits, pv_chunk, 0)

    acc = acc_ref[...]
    out = acc[:, :HEAD_DIM] / acc[:, HEAD_DIM:HEAD_DIM + 1]
    for h in range(A_HEADS):
        o_ref[:, h * HEAD_DIM:(h + 1) * HEAD_DIM] = out[h * qb:(h + 1) * qb].astype(o_ref.dtype)


def _dsa(proj3, kc=512):
    b, seq, _ = proj3.shape
    top_k = min(IDX_TOPK_MAX, seq // 4)
    kc = min(kc, seq)
    assert seq % kc == 0 and kc % Q_BLOCK == 0 and kc >= top_k
    kern = functools.partial(_dsa_kernel, seq=seq, kc=kc, top_k=top_k)
    hq = A_HEADS * Q_BLOCK
    return pl.pallas_call(
        kern,
        grid=(b, seq // Q_BLOCK),
        in_specs=[
            pl.BlockSpec((None, Q_BLOCK, A_HEADS * HEAD_DIM), lambda bb, i: (bb, i, COL_AQ // 512)),
            pl.BlockSpec((None, Q_BLOCK, IDX_HEADS * IDX_DIM), lambda bb, i: (bb, i, COL_IQ // 256)),
            pl.BlockSpec((None, seq, LANES), lambda bb, i: (bb, 0, COL_AKIK // LANES)),
            pl.BlockSpec((None, seq, LANES), lambda bb, i: (bb, 0, COL_AVIW // LANES)),
        ],
        out_specs=pl.BlockSpec((None, Q_BLOCK, A_HEADS * HEAD_DIM), lambda bb, i: (bb, i, 0)),
        out_shape=jax.ShapeDtypeStruct((b, seq, A_HEADS * HEAD_DIM), BF16),
        scratch_shapes=[
            pltpu.VMEM((seq, HEAD_DIM), BF16),
            pltpu.VMEM((seq, LANES), BF16),
            pltpu.VMEM((seq, 3 * IDX_DIM), BF16),
            pltpu.VMEM((seq // kc, Q_BLOCK, kc), I32),
            pltpu.VMEM((hq, HEAD_DIM), BF16),
            pltpu.VMEM((hq, LANES), F32),
            pltpu.VMEM((hq, LANES), F32),
            pltpu.VMEM((kc, kc), BF16),
            pltpu.VMEM((kc, LANES), BF16),
            pltpu.VMEM((2, hq, kc), F32),
        ],
        compiler_params=_cparams(2),
        name="dsa_topk_attention",
    )(proj3, proj3, proj3, proj3)


def _swa_kernel(sink_ref, q_ref, kp_ref, kc_ref, vp_ref, vc_ref, o_ref):
    i = pl.program_id(1)
    w = WINDOW
    group = B_HEADS // B_KV_HEADS
    q = q_ref[...]
    k = jnp.concatenate([kp_ref[...], kc_ref[...]], axis=0)
    v = jnp.concatenate([vp_ref[...], vc_ref[...]], axis=0)
    row = lax.broadcasted_iota(I32, (w, 2 * w), 0)
    col = lax.broadcasted_iota(I32, (w, 2 * w), 1)
    kpos = (i - 1) * w + col
    diff = i * w + row - kpos
    mask = jnp.logical_and(jnp.logical_and(diff >= 0, diff < w), kpos >= 0)
    for g in range(B_KV_HEADS):
        kg = k[:, g * HEAD_DIM:(g + 1) * HEAD_DIM].astype(BF16)
        vg = v[:, g * HEAD_DIM:(g + 1) * HEAD_DIM].astype(BF16)
        for hh in range(group):
            h = g * group + hh
            qh = (q[:, h * HEAD_DIM:(h + 1) * HEAD_DIM] * (HEAD_DIM ** -0.5)).astype(BF16)
            s = jnp.where(mask, _nt_dot(qh, kg), NEG_BIG)
            sink = sink_ref[h]
            m = jnp.maximum(jnp.max(s, axis=1, keepdims=True), sink)
            p = jnp.exp(s - m)
            den = jnp.sum(p, axis=1, keepdims=True) + jnp.exp(sink - m)
            o = jnp.dot(p.astype(BF16), vg, preferred_element_type=F32) / den
            o_ref[:, h * HEAD_DIM:(h + 1) * HEAD_DIM] = o.astype(o_ref.dtype)


def _swa(proj3, sinks):
    b, seq, _ = proj3.shape
    w = WINDOW
    kvw = B_KV_HEADS * HEAD_DIM
    assert kvw == LANES and seq % w == 0
    prev = lambda col: (lambda bb, i: (bb, jnp.maximum(i - 1, 0), col))
    cur = lambda col: (lambda bb, i: (bb, i, col))
    return pl.pallas_call(
        _swa_kernel,
        grid=(b, seq // w),
        in_specs=[
            pl.BlockSpec(memory_space=pltpu.SMEM),
            pl.BlockSpec((None, w, B_HEADS * HEAD_DIM), cur(COL_BQ // 512)),
            pl.BlockSpec((None, w, kvw), prev(COL_BK // LANES)),
            pl.BlockSpec((None, w, kvw), cur(COL_BK // LANES)),
            pl.BlockSpec((None, w, kvw), prev(COL_BV // LANES)),
            pl.BlockSpec((None, w, kvw), cur(COL_BV // LANES)),
        ],
        out_specs=pl.BlockSpec((None, w, B_HEADS * HEAD_DIM), lambda bb, i: (bb, i, 0)),
        out_shape=jax.ShapeDtypeStruct((b, seq, B_HEADS * HEAD_DIM), BF16),
        compiler_params=_cparams(2),
        name="swa_sink_attention",
    )(sinks, proj3, proj3, proj3, proj3, proj3)


def _moba_kernel(q_ref, k_ref, v_ref, o_ref,
                 kaug_ref, vaug_ref, kmean_ref, kmw_ref, qaug_ref, mrow_ref, acc_ref, sbuf_ref,
                 *, seq, tq, n_sel):
    i = pl.program_id(2)
    blk = MOBA_BLOCK
    nkb = seq // blk
    heads = LANES // HEAD_DIM
    assert heads == 2
    ones_lane = (HEAD_DIM, 0)

    @pl.when(i == 0)
    def _prep():
        kmean_ref[...] = jnp.zeros(kmean_ref.shape, F32)
        lane_b = lax.broadcasted_iota(I32, (blk, LANES), 1)

        def body(n, carry):
            rows = pl.ds(pl.multiple_of(n * blk, blk), blk)
            kb = k_ref[rows, :]
            vb = v_ref[rows, :]
            onehot = jnp.where(lane_b == n, 1.0, 0.0).astype(BF16)
            for e in range(heads):
                mine = (lane_b // HEAD_DIM) == e
                kaug_ref[e, rows, :] = jnp.concatenate(
                    [onehot, jnp.where(mine, kb, 0.0).astype(BF16)], axis=1)
                ones = jnp.where(lane_b == ones_lane[e], 1.0, 0.0)
                vaug_ref[e, rows, :] = jnp.where(mine, vb, ones).astype(BF16)
            kmean_ref[pl.ds(n, 1), :] = jnp.sum(kb, axis=0, keepdims=True) * (1.0 / blk)
            return carry
        lax.fori_loop(0, nkb, body, 0)
        km = kmean_ref[...]
        lane_m = lax.broadcasted_iota(I32, km.shape, 1)
        kmw = jnp.concatenate(
            [jnp.where((lane_m // HEAD_DIM) == e, km, 0.0) for e in range(heads)], axis=0)
        hi = kmw.astype(BF16)
        kmw_ref[0] = hi
        kmw_ref[1] = (kmw - hi.astype(F32)).astype(BF16)

    q = q_ref[...]
    own1 = (i * tq + lax.broadcasted_iota(I32, (tq, 1), 0)) // blk
    lane = lax.broadcasted_iota(I32, (tq, LANES), 1)
    lanef = lane.astype(F32)

    qhi = q.astype(BF16)
    qlo = (q - qhi.astype(F32)).astype(BF16)
    gates = (_nt_dot(qhi, kmw_ref[0]) + _nt_dot(qhi, kmw_ref[1])) + _nt_dot(qlo, kmw_ref[0])
    q2 = (q * Q_SCALE_LOG2).astype(BF16)
    for e in range(heads):
        g = jnp.where(lane < own1, gates[:, e * LANES:(e + 1) * LANES], -jnp.inf)
        allowed = lane == own1
        for _ in range(n_sel):
            gmax = jnp.max(g, axis=1, keepdims=True)
            is_max = jnp.logical_and(g == gmax, gmax > -jnp.inf)
            first = jnp.min(jnp.where(is_max, lanef, float(LANES)), axis=1, keepdims=True)
            onehot = lanef == first
            allowed = jnp.logical_or(allowed, onehot)
            g = jnp.where(onehot, -jnp.inf, g)
        bias = jnp.where(allowed, 0.0, NEG_BIG).astype(BF16)
        qaug_ref[e] = jnp.concatenate([bias, q2], axis=1)

    kc = tq
    ngrp = kc // LANES
    col = lax.broadcasted_iota(I32, (tq, kc), 1)
    row = lax.broadcasted_iota(I32, (tq, kc), 0)

    def chunk_rows(c):
        return pl.ds(pl.multiple_of(c * kc, kc), kc)

    def logits(c, slot):
        for e in range(heads):
            sbuf_ref[slot, e] = _nt_dot(qaug_ref[e], kaug_ref[e, chunk_rows(c), :])

    def masked(e, slot, diagonal):
        s = sbuf_ref[slot, e]
        return jnp.where(col <= row, s, NEG_BIG) if diagonal else s

    def max_chunk(c, slot, carry, is_last):
        for e in range(heads):
            s = masked(e, slot, is_last)
            m = mrow_ref[e]
            for gi in range(ngrp):
                m = jnp.maximum(m, s[:, gi * LANES:(gi + 1) * LANES])
            mrow_ref[e] = m
        return carry
    mrow_ref[...] = jnp.full(mrow_ref.shape, NEG_BIG, F32)
    _pipelined(i + 1, logits, max_chunk, 0)
    for e in range(heads):
        mrow_ref[e] = jnp.broadcast_to(jnp.max(mrow_ref[e], axis=1, keepdims=True), (tq, LANES))

    def pv_chunk(c, slot, carry, is_last):
        for e in range(heads):
            s = masked(e, slot, is_last)
            m = mrow_ref[e]
            p = jnp.concatenate(
                [jnp.exp2(s[:, gi * LANES:(gi + 1) * LANES] - m).astype(BF16) for gi in range(ngrp)],
                axis=1)
            acc_ref[e] += jnp.dot(p, vaug_ref[e, chunk_rows(c), :], preferred_element_type=F32)
        return carry
    acc_ref[...] = jnp.zeros(acc_ref.shape, F32)
    _pipelined(i + 1, logits, pv_chunk, 0)

    outs = []
    for e in range(heads):
        acc = acc_ref[e]
        outs.append(acc / acc[:, ones_lane[e]:ones_lane[e] + 1])
    o_ref[...] = jnp.where(lane < HEAD_DIM, outs[0], outs[1]).astype(o_ref.dtype)


def _moba(proj3, tq=512):
    b, seq, _ = proj3.shape
    tq = min(tq, seq)
    blk = MOBA_BLOCK
    assert seq % blk == 0 and tq % blk == 0 and seq % tq == 0
    nkb = seq // blk
    assert nkb <= LANES
    n_sel = min(MOBA_TOPK, nkb - 1)
    heads = LANES // HEAD_DIM
    kern = functools.partial(_moba_kernel, seq=seq, tq=tq, n_sel=n_sel)
    return pl.pallas_call(
        kern,
        grid=(b, C_HEADS // heads, seq // tq),
        in_specs=[
            pl.BlockSpec((None, tq, LANES), lambda bb, hp, i: (bb, i, COL_CQ // LANES + hp)),
            pl.BlockSpec((None, seq, LANES), lambda bb, hp, i: (bb, 0, COL_CK // LANES + hp)),
            pl.BlockSpec((None, seq, LANES), lambda bb, hp, i: (bb, 0, COL_CV // LANES + hp)),
        ],
        out_specs=pl.BlockSpec((None, tq, LANES), lambda bb, hp, i: (bb, i, hp)),
        out_shape=jax.ShapeDtypeStruct((b, seq, C_HEADS * HEAD_DIM), BF16),
        scratch_shapes=[
            pltpu.VMEM((heads, seq, 2 * LANES), BF16),
            pltpu.VMEM((heads, seq, LANES), BF16),
            pltpu.VMEM((LANES, LANES), F32),
            pltpu.VMEM((2, heads * LANES, LANES), BF16),
            pltpu.VMEM((heads, tq, 2 * LANES), BF16),
            pltpu.VMEM((heads, tq, LANES), F32),
            pltpu.VMEM((heads, tq, LANES), F32),
            pltpu.VMEM((2, heads, tq, tq), F32),
        ],
        compiler_params=_cparams(3),
        name="moba_attention",
    )(proj3, proj3, proj3)


def _merge_kernel(x_ref, oa_ref, ob_ref, oc_ref, g_ref, wb_ref, wo_ref, o_ref):
    merged = None
    for n, br in enumerate((oa_ref, ob_ref, oc_ref)):
        y = jnp.dot(br[...], wb_ref[n], preferred_element_type=F32)
        term = jax.nn.sigmoid(g_ref[:, n * D_MODEL:(n + 1) * D_MODEL]) * y
        merged = term if merged is None else merged + term
    o_ref[...] = x_ref[...] + jnp.dot(merged.astype(BF16), wo_ref[...], preferred_element_type=F32)


def _merge(x2, oa, ob, oc, proj2, w_branch, w_out, tm=512):
    t = x2.shape[0]
    assert t % tm == 0
    row = lambda i: (i, 0)
    return pl.pallas_call(
        _merge_kernel,
        grid=(t // tm,),
        in_specs=[
            pl.BlockSpec((tm, D_MODEL), row),
            pl.BlockSpec((tm, BRANCH_WIDTH), row),
            pl.BlockSpec((tm, BRANCH_WIDTH), row),
            pl.BlockSpec((tm, BRANCH_WIDTH), row),
            pl.BlockSpec((tm, N_BRANCH * D_MODEL), lambda i: (i, COL_GATES // (N_BRANCH * D_MODEL))),
            pl.BlockSpec((N_BRANCH, BRANCH_WIDTH, D_MODEL), lambda i: (0, 0, 0)),
            pl.BlockSpec((D_MODEL, D_MODEL), lambda i: (0, 0)),
        ],
        out_specs=pl.BlockSpec((tm, D_MODEL), row),
        out_shape=jax.ShapeDtypeStruct((t, D_MODEL), F32),
        compiler_params=_cparams(1),
        name="gated_merge_out_proj",
    )(x2, oa, ob, oc, proj2, w_branch, w_out)


def _ffn_kernel(x_ref, g_ref, w1_ref, w2_ref, gf_ref, o_ref, h_ref, acc_ref, *, final_norm):
    j = pl.program_id(1)

    @pl.when(j == 0)
    def _():
        x = x_ref[...]
        ms = jnp.mean(x * x, axis=-1, keepdims=True)
        h_ref[...] = (x * lax.rsqrt(ms + NORM_EPS) * g_ref[...]).astype(BF16)
        acc_ref[...] = jnp.zeros(acc_ref.shape, F32)

    u = jnp.maximum(jnp.dot(h_ref[...], w1_ref[...], preferred_element_type=F32), 0.0)
    acc_ref[...] += jnp.dot((u * u).astype(BF16), w2_ref[...], preferred_element_type=F32)

    @pl.when(j == pl.num_programs(1) - 1)
    def _():
        y = x_ref[...] + acc_ref[...]
        if final_norm:
            ms = jnp.mean(y * y, axis=-1, keepdims=True)
            y = y * lax.rsqrt(ms + NORM_EPS) * gf_ref[...]
        o_ref[...] = y


def _ffn(x2, gain, w1, w2, gain_f, final_norm, tm=512, tf=1024):
    t = x2.shape[0]
    assert t % tm == 0 and D_FF % tf == 0
    kern = functools.partial(_ffn_kernel, final_norm=final_norm)
    return pl.pallas_call(
        kern,
        grid=(t // tm, D_FF // tf),
        in_specs=[
            pl.BlockSpec((tm, D_MODEL), lambda i, j: (i, 0)),
            pl.BlockSpec((1, D_MODEL), lambda i, j: (0, 0)),
            pl.BlockSpec((D_MODEL, tf), lambda i, j: (0, j)),
            pl.BlockSpec((tf, D_MODEL), lambda i, j: (j, 0)),
            pl.BlockSpec((1, D_MODEL), lambda i, j: (0, 0)),
        ],
        out_specs=pl.BlockSpec((tm, D_MODEL), lambda i, j: (i, 0)),
        out_shape=jax.ShapeDtypeStruct((t, D_MODEL), F32),
        scratch_shapes=[pltpu.VMEM((tm, D_MODEL), BF16), pltpu.VMEM((tm, D_MODEL), F32)],
        compiler_params=_cparams(2),
        name="relu2_mlp",
    )(x2, gain, w1, w2, gain_f)


def _pack_w_in(w_in):
    sizes = (A_HEADS * HEAD_DIM, HEAD_DIM, HEAD_DIM, IDX_HEADS * IDX_DIM, IDX_DIM, IDX_HEADS,
             B_HEADS * HEAD_DIM, B_KV_HEADS * HEAD_DIM, B_KV_HEADS * HEAD_DIM,
             C_HEADS * HEAD_DIM, C_HEADS * HEAD_DIM, C_HEADS * HEAD_DIM, N_BRANCH * D_MODEL)
    offs = np.concatenate([[0], np.cumsum(sizes)])
    aq, ak, av, iq, ik, iw, bq, bk, bv, cq, ck, cv, gates = [
        w_in[..., offs[n]:offs[n + 1]] for n in range(len(sizes))]
    lead = w_in.shape[:-1]
    pad_iw = jnp.zeros(lead + (LANES - HEAD_DIM - IDX_HEADS,), w_in.dtype)
    parts = [gates, aq, bq, cq, ck, iq, bk, ak, ik, av, iw, pad_iw, bv, cv]
    used = sum(p.shape[-1] for p in parts)
    parts.append(jnp.zeros(lead + (N_PACKED - used,), w_in.dtype))
    return jnp.concatenate(parts, axis=-1).astype(BF16)


def _rope_tables(seq):
    half = HEAD_DIM // 2
    inv = 1.0 / (ROPE_THETA ** (jnp.arange(0, HEAD_DIM, 2, dtype=F32) / HEAD_DIM))
    ang = jnp.arange(seq, dtype=F32)[:, None] * inv[None, :]
    cos, sin = jnp.cos(ang), jnp.sin(ang)
    reps = LANES // HEAD_DIM
    cos_t = jnp.tile(jnp.concatenate([cos, cos], axis=1), (1, reps))
    sin_t = jnp.tile(jnp.concatenate([-sin, sin], axis=1), (1, reps))
    assert cos_t.shape == (seq, LANES) and half * 2 == HEAD_DIM
    return cos_t, sin_t


def kernel(x, norm1, w_in, sinks, w_branch, w_out, norm2, w_ff1, w_ff2, norm_f):
    b, seq, d = x.shape
    depth = w_in.shape[0]
    cos_t, sin_t = _rope_tables(seq)
    w_in_p = _pack_w_in(w_in)
    w_branch_b = w_branch.astype(BF16)
    w_out_b = w_out.astype(BF16)
    w_ff1_b = w_ff1.astype(BF16)
    w_ff2_b = w_ff2.astype(BF16)
    gain_f = norm_f.reshape(1, d)

    x2 = x.reshape(b * seq, d)
    for l in range(depth):
        proj2 = _project(x2, norm1[l].reshape(1, d), w_in_p[l], cos_t, sin_t, seq)
        proj3 = proj2.reshape(b, seq, N_PACKED)
        o_a = _dsa(proj3).reshape(b * seq, BRANCH_WIDTH)
        o_b = _swa(proj3, sinks[l]).reshape(b * seq, BRANCH_WIDTH)
        o_c = _moba(proj3).reshape(b * seq, BRANCH_WIDTH)
        x2 = _merge(x2, o_a, o_b, o_c, proj2, w_branch_b[l], w_out_b[l])
        x2 = _ffn(x2, norm2[l].reshape(1, d), w_ff1_b[l], w_ff2_b[l], gain_f,
                  final_norm=(l == depth - 1))
    return x2.reshape(b, seq, d)
```

```python
import functools

import jax
import jax.numpy as jnp
import numpy as np
from jax import lax
from jax.experimental import pallas as pl
from jax.experimental.pallas import tpu as pltpu

F32 = jnp.float32
BF16 = jnp.bfloat16
I32 = jnp.int32

D_MODEL = 1024
HEAD_DIM = 64
ROPE_THETA = 10000.0
NORM_EPS = 1e-6
A_HEADS = 8
IDX_HEADS = 4
IDX_DIM = 64
IDX_TOPK_MAX = 256
B_HEADS = 8
B_KV_HEADS = 2
WINDOW = 128
C_HEADS = 8
MOBA_BLOCK = 256
MOBA_TOPK = 3
Q_BLOCK = 128
N_BRANCH = 3
BRANCH_WIDTH = 8 * HEAD_DIM
D_FF = 4 * D_MODEL

LANES = 128
NEG_BIG = -1e30
Q_SCALE_LOG2 = (HEAD_DIM ** -0.5) * float(np.log2(np.e))
INT_MIN = -(2 ** 31)

COL_GATES = 0
COL_AQ = 3072
COL_BQ = 3584
COL_CQ = 4096
COL_CK = 4608
COL_IQ = 5120
COL_BK = 5376
COL_AKIK = 5504
ROPE_BEGIN, ROPE_END = 3072, 5632
COL_AVIW = 5632
COL_BV = 5760
COL_CV = 5888
N_PACKED = 6656

VMEM_LIMIT = 56 * 1024 * 1024


def _cparams(n_axes):
    return pltpu.CompilerParams(dimension_semantics=("arbitrary",) * n_axes,
                                vmem_limit_bytes=VMEM_LIMIT)


def _nt_dot(a, b):
    return lax.dot_general(a, b, (((1,), (1,)), ((), ())), preferred_element_type=F32)


def _pipelined(n, produce, consume, init):
    produce(0, 0)

    def pair(t, carry):
        c = 2 * t
        produce(c + 1, 1)
        carry = consume(c, 0, carry, False)
        produce(c + 2, 0)
        return consume(c + 1, 1, carry, False)
    n_pairs = (n - 1) // 2
    carry = lax.fori_loop(0, n_pairs, pair, init)
    c0 = 2 * n_pairs

    def two_left(carry):
        produce(c0 + 1, 1)
        carry = consume(c0, 0, carry, False)
        return consume(c0 + 1, 1, carry, True)

    def one_left(carry):
        return consume(c0, 0, carry, True)
    return lax.cond(n - c0 == 2, two_left, one_left, carry)


def _split3_lhs(x):
    hi = x.astype(BF16)
    lo = (x - hi.astype(F32)).astype(BF16)
    return jnp.concatenate([hi, hi, lo], axis=1)


def _split3_rhs(x):
    hi = x.astype(BF16)
    lo = (x - hi.astype(F32)).astype(BF16)
    return jnp.concatenate([hi, lo, hi], axis=1)


def _proj_kernel(x_ref, g_ref, w_ref, cos_ref, sin_ref, o_ref, h_ref, *, rope_lo, rope_hi, tn):
    j = pl.program_id(1)

    @pl.when(j == 0)
    def _():
        x = x_ref[...]
        ms = jnp.mean(x * x, axis=-1, keepdims=True)
        h_ref[...] = (x * lax.rsqrt(ms + NORM_EPS) * g_ref[...]).astype(BF16)

    acc = jnp.dot(h_ref[...], w_ref[...], preferred_element_type=F32)
    is_rope = jnp.logical_and(j >= rope_lo, j < rope_hi)

    @pl.when(is_rope)
    def _():
        cos = cos_ref[...]
        sin = sin_ref[...]
        lane = lax.broadcasted_iota(I32, cos.shape, 1)
        first_half = (lane % HEAD_DIM) < (HEAD_DIM // 2)
        for c in range(tn // LANES):
            a = acc[:, c * LANES:(c + 1) * LANES]
            partner = jnp.where(first_half,
                                pltpu.roll(a, LANES - HEAD_DIM // 2, 1),
                                pltpu.roll(a, HEAD_DIM // 2, 1))
            o_ref[:, c * LANES:(c + 1) * LANES] = a * cos + partner * sin

    @pl.when(jnp.logical_not(is_rope))
    def _():
        o_ref[...] = acc


def _project(x2, gain, w_packed, cos_t, sin_t, seq, tm=1024, tn=512):
    t = x2.shape[0]
    tm = min(tm, seq)
    n = w_packed.shape[1]
    assert t % tm == 0 and seq % tm == 0 and n % tn == 0
    assert ROPE_BEGIN % tn == 0 and ROPE_END % tn == 0
    pos_blocks = seq // tm
    kern = functools.partial(_proj_kernel, rope_lo=ROPE_BEGIN // tn, rope_hi=ROPE_END // tn, tn=tn)
    return pl.pallas_call(
        kern,
        grid=(t // tm, n // tn),
        in_specs=[
            pl.BlockSpec((tm, D_MODEL), lambda i, j: (i, 0)),
            pl.BlockSpec((1, D_MODEL), lambda i, j: (0, 0)),
            pl.BlockSpec((D_MODEL, tn), lambda i, j: (0, j)),
            pl.BlockSpec((tm, LANES), lambda i, j: (i % pos_blocks, 0)),
            pl.BlockSpec((tm, LANES), lambda i, j: (i % pos_blocks, 0)),
        ],
        out_specs=pl.BlockSpec((tm, tn), lambda i, j: (i, j)),
        out_shape=jax.ShapeDtypeStruct((t, n), F32),
        scratch_shapes=[pltpu.VMEM((tm, D_MODEL), BF16)],
        compiler_params=_cparams(2),
        name="norm_proj_rope",
    )(x2, gain, w_packed, cos_t, sin_t)


def _value_with_ones(v):
    lane = lax.broadcasted_iota(I32, v.shape, 1)
    return jnp.concatenate([v.astype(BF16), jnp.where(lane == 0, 1.0, 0.0).astype(BF16)], axis=1)


def _dsa_kernel(aq_ref, iq_ref, kk_ref, vw_ref, o_ref,
                kbf_ref, vbf_ref, ikc_ref, keys_ref, qall_ref, mrow_ref, acc_ref, tri_ref, ones_ref,
                sbuf_ref, *, seq, kc, top_k):
    i = pl.program_id(1)
    qb = Q_BLOCK
    n_prep = seq // kc
    idx_scale = (IDX_DIM ** -0.5) * (IDX_HEADS ** -0.5)

    @pl.when(i == 0)
    def _prep():
        def body(c, carry):
            rows = pl.ds(pl.multiple_of(c * kc, kc), kc)
            kk = kk_ref[rows, :]
            kbf_ref[rows, :] = kk[:, :HEAD_DIM].astype(BF16)
            ikc_ref[rows, :] = _split3_rhs(kk[:, HEAD_DIM:])
            vbf_ref[rows, :] = _value_with_ones(vw_ref[rows, :HEAD_DIM])
            return carry
        lax.fori_loop(0, n_prep, body, 0)
        r = lax.broadcasted_iota(I32, (kc, kc), 0)
        cidx = lax.broadcasted_iota(I32, (kc, kc), 1)
        tri_ref[...] = jnp.where(r <= cidx, 1.0, 0.0).astype(BF16)
        ones_ref[...] = jnp.ones(ones_ref.shape, BF16)

    nkc = (i * qb) // kc + 1

    iq = iq_ref[...]
    w = vw_ref[pl.ds(pl.multiple_of(i * qb, qb), qb), HEAD_DIM:HEAD_DIM + IDX_HEADS] * idx_scale
    qcat = jnp.concatenate(
        [_split3_lhs(iq[:, h * IDX_DIM:(h + 1) * IDX_DIM]) for h in range(IDX_HEADS)], axis=0)
    row = lax.broadcasted_iota(I32, (qb, LANES), 0)
    lane = lax.broadcasted_iota(I32, (qb, LANES), 1)
    qpos = i * qb + row
    w_b = [jnp.broadcast_to(w[:, h:h + 1], (qb, LANES)) for h in range(IDX_HEADS)]
    ngrp = kc // LANES

    def chunk_rows(c):
        return pl.ds(pl.multiple_of(c * kc, kc), kc)

    def to_key(x):
        bits = lax.bitcast_convert_type(x, I32)
        return jnp.where(bits < 0, bits ^ jnp.int32(0x7FFFFFFF), bits)

    def idx_dots(c, slot):
        sbuf_ref[slot, :IDX_HEADS * qb] = _nt_dot(qcat, ikc_ref[chunk_rows(c), :])

    def score_chunk(c, slot, smax, is_last):
        for g in range(ngrp):
            sc = None
            for h in range(IDX_HEADS):
                d = sbuf_ref[slot, h * qb:(h + 1) * qb, g * LANES:(g + 1) * LANES]
                term = jnp.maximum(d, 0.0) * w_b[h]
                sc = term if sc is None else sc + term
            sc = jnp.where(sc == 0.0, 0.0, sc)
            causal = c * kc + g * LANES + lane <= qpos
            keys_ref[c, :, g * LANES:(g + 1) * LANES] = jnp.where(causal, to_key(sc), jnp.int32(INT_MIN))
            smax = jnp.maximum(smax, jnp.where(causal, sc, -jnp.inf))
        return smax

    smax = _pipelined(nkc, idx_dots, score_chunk, jnp.full((qb, LANES), -jnp.inf, F32))
    key_max = to_key(jnp.max(smax, axis=1, keepdims=True))

    def count(pred, thr):
        thr_b = jnp.broadcast_to(thr, (qb, LANES))

        def body(c, acc):
            kch = keys_ref[c]
            for g in range(kc // LANES):
                acc = acc + jnp.where(pred(kch[:, g * LANES:(g + 1) * LANES], thr_b), 1.0, 0.0)
            return acc
        acc = lax.fori_loop(0, nkc, body, jnp.zeros((qb, LANES), F32))
        return jnp.sum(acc, axis=1, keepdims=True)

    ge = lambda k, t: k >= t
    kf = float(top_k)

    def probe(v, state):
        lo, n_lo, hi = state
        n_v = count(ge, v)
        inside = jnp.logical_and(v > lo, v < hi)
        up = jnp.logical_and(inside, n_v >= kf)
        down = jnp.logical_and(inside, n_v < kf)
        return jnp.where(up, v, lo), jnp.where(up, n_v, n_lo), jnp.where(down, v, hi)

    def unfinished(state):
        lo, n_lo, hi = state
        done = jnp.logical_or(n_lo == kf, hi - 1 == lo)
        return jnp.max(jnp.where(done, 0.0, 1.0)) > 0.0

    int_min = jnp.full((qb, 1), INT_MIN, I32)
    state = (int_min, jnp.full((qb, 1), float(seq + 1), F32), key_max + 1)
    state = probe(jnp.ones((qb, 1), I32), state)
    state = probe(jnp.zeros((qb, 1), I32), state)
    two_binades = jnp.int32(1 << 24)
    state = probe(jnp.maximum(key_max, jnp.int32(INT_MIN) + two_binades) - two_binades, state)

    def bisect_cond(carry):
        return jnp.logical_and(carry[0], carry[1] < 34)

    def bisect_body(carry):
        _, n_iter, lo, n_lo, hi = carry
        mid = lo + lax.shift_right_logical(hi - lo, jnp.int32(1))
        state = probe(mid, (lo, n_lo, hi))
        return (unfinished(state), n_iter + 1) + state
    carry = lax.while_loop(bisect_cond, bisect_body, (unfinished(state), jnp.int32(0)) + state)
    thr = carry[2]

    n_gt = count(lambda k, t: k > t, thr)
    need = jnp.where(thr > jnp.int32(INT_MIN), kf - n_gt, 0.0)

    aq = aq_ref[...]
    for h in range(A_HEADS):
        qall_ref[h * qb:(h + 1) * qb, :] = (
            aq[:, h * HEAD_DIM:(h + 1) * HEAD_DIM] * Q_SCALE_LOG2).astype(BF16)
    thr_b = jnp.broadcast_to(thr, (qb, LANES))
    need_b = jnp.broadcast_to(need, (qb, LANES))

    def logits(c, slot):
        sbuf_ref[slot] = _nt_dot(qall_ref[...], kbf_ref[chunk_rows(c), :])

    def masked(slot, bias):
        return (sbuf_ref[slot].reshape(A_HEADS, qb, kc) + bias[None]).reshape(A_HEADS * qb, kc)

    def bias_chunk(c, ties_before):
        kch = keys_ref[c]
        eq = [kch[:, g * LANES:(g + 1) * LANES] == thr_b for g in range(ngrp)]
        eq_b = jnp.concatenate([jnp.where(e, 1.0, 0.0).astype(BF16) for e in eq], axis=1)
        rank = jnp.dot(eq_b, tri_ref[...], preferred_element_type=F32)
        bias = jnp.concatenate(
            [jnp.where(jnp.logical_or(
                kch[:, g * LANES:(g + 1) * LANES] > thr_b,
                jnp.logical_and(eq[g], rank[:, g * LANES:(g + 1) * LANES] + ties_before <= need_b)),
                0.0, NEG_BIG) for g in range(ngrp)], axis=1)
        keys_ref[c] = lax.bitcast_convert_type(bias, I32)
        return ties_before + jnp.dot(eq_b, ones_ref[...], preferred_element_type=F32)
    lax.fori_loop(0, nkc, bias_chunk, jnp.zeros((qb, LANES), F32))

    def att_chunk(c, slot, carry, is_last):
        s = masked(slot, lax.bitcast_convert_type(keys_ref[c], F32))
        m_old = mrow_ref[...]
        mx = s[:, :LANES]
        for g in range(1, ngrp):
            mx = jnp.maximum(mx, s[:, g * LANES:(g + 1) * LANES])
        m_new = jnp.maximum(m_old, jnp.broadcast_to(jnp.max(mx, axis=1, keepdims=True), m_old.shape))
        p = jnp.concatenate(
            [jnp.exp2(s[:, g * LANES:(g + 1) * LANES] - m_new).astype(BF16) for g in range(ngrp)], axis=1)
        acc_ref[...] = jnp.exp2(m_old - m_new) * acc_ref[...] + jnp.dot(
            p, vbf_ref[chunk_rows(c), :], preferred_element_type=F32)
        mrow_ref[...] = m_new
        return carry
    mrow_ref[...] = jnp.full(mrow_ref.shape, NEG_BIG, F32)
    acc_ref[...] = jnp.zeros(acc_ref.shape, F32)
    _pipelined(nkc, logits, att_chunk, 0)

    acc = acc_ref[...]
    out = acc[:, :HEAD_DIM] / acc[:, HEAD_DIM:HEAD_DIM + 1]
    for h in range(A_HEADS):
        o_ref[:, h * HEAD_DIM:(h + 1) * HEAD_DIM] = out[h * qb:(h + 1) * qb].astype(o_ref.dtype)


def _dsa(proj3, kc=512):
    b, seq, _ = proj3.shape
    top_k = min(IDX_TOPK_MAX, seq // 4)
    kc = min(kc, seq)
    assert seq % kc == 0 and kc % Q_BLOCK == 0 and kc >= top_k
    kern = functools.partial(_dsa_kernel, seq=seq, kc=kc, top_k=top_k)
    hq = A_HEADS * Q_BLOCK
    return pl.pallas_call(
        kern,
        grid=(b, seq // Q_BLOCK),
        in_specs=[
            pl.BlockSpec((None, Q_BLOCK, A_HEADS * HEAD_DIM), lambda bb, i: (bb, i, COL_AQ // 512)),
            pl.BlockSpec((None, Q_BLOCK, IDX_HEADS * IDX_DIM), lambda bb, i: (bb, i, COL_IQ // 256)),
            pl.BlockSpec((None, seq, LANES), lambda bb, i: (bb, 0, COL_AKIK // LANES)),
            pl.BlockSpec((None, seq, LANES), lambda bb, i: (bb, 0, COL_AVIW // LANES)),
        ],
        out_specs=pl.BlockSpec((None, Q_BLOCK, A_HEADS * HEAD_DIM), lambda bb, i: (bb, i, 0)),
        out_shape=jax.ShapeDtypeStruct((b, seq, A_HEADS * HEAD_DIM), BF16),
        scratch_shapes=[
            pltpu.VMEM((seq, HEAD_DIM), BF16),
            pltpu.VMEM((seq, LANES), BF16),
            pltpu.VMEM((seq, 3 * IDX_DIM), BF16),
            pltpu.VMEM((seq // kc, Q_BLOCK, kc), I32),
            pltpu.VMEM((hq, HEAD_DIM), BF16),
            pltpu.VMEM((hq, LANES), F32),
            pltpu.VMEM((hq, LANES), F32),
            pltpu.VMEM((kc, kc), BF16),
            pltpu.VMEM((kc, LANES), BF16),
            pltpu.VMEM((2, hq, kc), F32),
        ],
        compiler_params=_cparams(2),
        name="dsa_topk_attention",
    )(proj3, proj3, proj3, proj3)


def _swa_kernel(sink_ref, q_ref, kp_ref, kc_ref, vp_ref, vc_ref, o_ref):
    i = pl.program_id(1)
    w = WINDOW
    group = B_HEADS // B_KV_HEADS
    q = q_ref[...]
    k = jnp.concatenate([kp_ref[...], kc_ref[...]], axis=0)
    v = jnp.concatenate([vp_ref[...], vc_ref[...]], axis=0)
    row = lax.broadcasted_iota(I32, (w, 2 * w), 0)
    col = lax.broadcasted_iota(I32, (w, 2 * w), 1)
    kpos = (i - 1) * w + col
    diff = i * w + row - kpos
    mask = jnp.logical_and(jnp.logical_and(diff >= 0, diff < w), kpos >= 0)
    for g in range(B_KV_HEADS):
        kg = k[:, g * HEAD_DIM:(g + 1) * HEAD_DIM].astype(BF16)
        vg = v[:, g * HEAD_DIM:(g + 1) * HEAD_DIM].astype(BF16)
        for hh in range(group):
            h = g * group + hh
            qh = (q[:, h * HEAD_DIM:(h + 1) * HEAD_DIM] * (HEAD_DIM ** -0.5)).astype(BF16)
            s = jnp.where(mask, _nt_dot(qh, kg), NEG_BIG)
            sink = sink_ref[h]
            m = jnp.maximum(jnp.max(s, axis=1, keepdims=True), sink)
            p = jnp.exp(s - m)
            den = jnp.sum(p, axis=1, keepdims=True) + jnp.exp(sink - m)
            o = jnp.dot(p.astype(BF16), vg, preferred_element_type=F32) / den
            o_ref[:, h * HEAD_DIM:(h + 1) * HEAD_DIM] = o.astype(o_ref.dtype)


def _swa(proj3, sinks):
    b, seq, _ = proj3.shape
    w = WINDOW
    kvw = B_KV_HEADS * HEAD_DIM
    assert kvw == LANES and seq % w == 0
    prev = lambda col: (lambda bb, i: (bb, jnp.maximum(i - 1, 0), col))
    cur = lambda col: (lambda bb, i: (bb, i, col))
    return pl.pallas_call(
        _swa_kernel,
        grid=(b, seq // w),
        in_specs=[
            pl.BlockSpec(memory_space=pltpu.SMEM),
            pl.BlockSpec((None, w, B_HEADS * HEAD_DIM), cur(COL_BQ // 512)),
            pl.BlockSpec((None, w, kvw), prev(COL_BK // LANES)),
            pl.BlockSpec((None, w, kvw), cur(COL_BK // LANES)),
            pl.BlockSpec((None, w, kvw), prev(COL_BV // LANES)),
            pl.BlockSpec((None, w, kvw), cur(COL_BV // LANES)),
        ],
        out_specs=pl.BlockSpec((None, w, B_HEADS * HEAD_DIM), lambda bb, i: (bb, i, 0)),
        out_shape=jax.ShapeDtypeStruct((b, seq, B_HEADS * HEAD_DIM), BF16),
        compiler_params=_cparams(2),
        name="swa_sink_attention",
    )(sinks, proj3, proj3, proj3, proj3, proj3)


def _moba_kernel(q_ref, k_ref, v_ref, o_ref,
                 kaug_ref, vaug_ref, kmean_ref, kmw_ref, qaug_ref, mrow_ref, acc_ref, sbuf_ref,
                 *, seq, tq, n_sel):
    i = pl.program_id(2)
    blk = MOBA_BLOCK
    nkb = seq // blk
    heads = LANES // HEAD_DIM
    assert heads == 2
    ones_lane = (HEAD_DIM, 0)

    @pl.when(i == 0)
    def _prep():
        kmean_ref[...] = jnp.zeros(kmean_ref.shape, F32)
        lane_b = lax.broadcasted_iota(I32, (blk, LANES), 1)

        def body(n, carry):
            rows = pl.ds(pl.multiple_of(n * blk, blk), blk)
            kb = k_ref[rows, :]
            vb = v_ref[rows, :]
            onehot = jnp.where(lane_b == n, 1.0, 0.0).astype(BF16)
            for e in range(heads):
                mine = (lane_b // HEAD_DIM) == e
                kaug_ref[e, rows, :] = jnp.concatenate(
                    [onehot, jnp.where(mine, kb, 0.0).astype(BF16)], axis=1)
                ones = jnp.where(lane_b == ones_lane[e], 1.0, 0.0)
                vaug_ref[e, rows, :] = jnp.where(mine, vb, ones).astype(BF16)
            kmean_ref[pl.ds(n, 1), :] = jnp.sum(kb, axis=0, keepdims=True) * (1.0 / blk)
            return carry
        lax.fori_loop(0, nkb, body, 0)
        km = kmean_ref[...]
        lane_m = lax.broadcasted_iota(I32, km.shape, 1)
        kmw = jnp.concatenate(
            [jnp.where((lane_m // HEAD_DIM) == e, km, 0.0) for e in range(heads)], axis=0)
        hi = kmw.astype(BF16)
        kmw_ref[0] = hi
        kmw_ref[1] = (kmw - hi.astype(F32)).astype(BF16)

    q = q_ref[...]
    own1 = (i * tq + lax.broadcasted_iota(I32, (tq, 1), 0)) // blk
    lane = lax.broadcasted_iota(I32, (tq, LANES), 1)
    lanef = lane.astype(F32)

    qhi = q.astype(BF16)
    qlo = (q - qhi.astype(F32)).astype(BF16)
    gates = (_nt_dot(qhi, kmw_ref[0]) + _nt_dot(qhi, kmw_ref[1])) + _nt_dot(qlo, kmw_ref[0])
    q2 = (q * Q_SCALE_LOG2).astype(BF16)
    for e in range(heads):
        g = jnp.where(lane < own1, gates[:, e * LANES:(e + 1) * LANES], -jnp.inf)
        allowed = lane == own1
        for _ in range(n_sel):
            gmax = jnp.max(g, axis=1, keepdims=True)
            is_max = jnp.logical_and(g == gmax, gmax > -jnp.inf)
            first = jnp.min(jnp.where(is_max, lanef, float(LANES)), axis=1, keepdims=True)
            onehot = lanef == first
            allowed = jnp.logical_or(allowed, onehot)
            g = jnp.where(onehot, -jnp.inf, g)
        bias = jnp.where(allowed, 0.0, NEG_BIG).astype(BF16)
        qaug_ref[e] = jnp.concatenate([bias, q2], axis=1)

    kc = tq
    ngrp = kc // LANES
    col = lax.broadcasted_iota(I32, (tq, kc), 1)
    row = lax.broadcasted_iota(I32, (tq, kc), 0)

    def chunk_rows(c):
        return pl.ds(pl.multiple_of(c * kc, kc), kc)

    def logits(c, slot):
        for e in range(heads):
            sbuf_ref[slot, e] = _nt_dot(qaug_ref[e], kaug_ref[e, chunk_rows(c), :])

    def masked(e, slot, diagonal):
        s = sbuf_ref[slot, e]
        return jnp.where(col <= row, s, NEG_BIG) if diagonal else s

    def att_chunk(c, slot, carry, is_last):
        for e in range(heads):
            s = masked(e, slot, is_last)
            m_old = mrow_ref[e]
            mx = s[:, :LANES]
            for gi in range(1, ngrp):
                mx = jnp.maximum(mx, s[:, gi * LANES:(gi + 1) * LANES])
            m_new = jnp.maximum(m_old, jnp.broadcast_to(jnp.max(mx, axis=1, keepdims=True), (tq, LANES)))
            p = jnp.concatenate(
                [jnp.exp2(s[:, gi * LANES:(gi + 1) * LANES] - m_new).astype(BF16) for gi in range(ngrp)],
                axis=1)
            acc_ref[e] = jnp.exp2(m_old - m_new) * acc_ref[e] + jnp.dot(
                p, vaug_ref[e, chunk_rows(c), :], preferred_element_type=F32)
            mrow_ref[e] = m_new
        return carry
    mrow_ref[...] = jnp.full(mrow_ref.shape, NEG_BIG, F32)
    acc_ref[...] = jnp.zeros(acc_ref.shape, F32)
    _pipelined(i + 1, logits, att_chunk, 0)

    outs = []
    for e in range(heads):
        acc = acc_ref[e]
        outs.append(acc / acc[:, ones_lane[e]:ones_lane[e] + 1])
    o_ref[...] = jnp.where(lane < HEAD_DIM, outs[0], outs[1]).astype(o_ref.dtype)


def _moba(proj3, tq=512):
    b, seq, _ = proj3.shape
    tq = min(tq, seq)
    blk = MOBA_BLOCK
    assert seq % blk == 0 and tq % blk == 0 and seq % tq == 0
    nkb = seq // blk
    assert nkb <= LANES
    n_sel = min(MOBA_TOPK, nkb - 1)
    heads = LANES // HEAD_DIM
    kern = functools.partial(_moba_kernel, seq=seq, tq=tq, n_sel=n_sel)
    return pl.pallas_call(
        kern,
        grid=(b, C_HEADS // heads, seq // tq),
        in_specs=[
            pl.BlockSpec((None, tq, LANES), lambda bb, hp, i: (bb, i, COL_CQ // LANES + hp)),
            pl.BlockSpec((None, seq, LANES), lambda bb, hp, i: (bb, 0, COL_CK // LANES + hp)),
            pl.BlockSpec((None, seq, LANES), lambda bb, hp, i: (bb, 0, COL_CV // LANES + hp)),
        ],
        out_specs=pl.BlockSpec((None, tq, LANES), lambda bb, hp, i: (bb, i, hp)),
        out_shape=jax.ShapeDtypeStruct((b, seq, C_HEADS * HEAD_DIM), BF16),
        scratch_shapes=[
            pltpu.VMEM((heads, seq, 2 * LANES), BF16),
            pltpu.VMEM((heads, seq, LANES), BF16),
            pltpu.VMEM((LANES, LANES), F32),
            pltpu.VMEM((2, heads * LANES, LANES), BF16),
            pltpu.VMEM((heads, tq, 2 * LANES), BF16),
            pltpu.VMEM((heads, tq, LANES), F32),
            pltpu.VMEM((heads, tq, LANES), F32),
            pltpu.VMEM((2, heads, tq, tq), F32),
        ],
        compiler_params=_cparams(3),
        name="moba_attention",
    )(proj3, proj3, proj3)


def _merge_kernel(x_ref, oa_ref, ob_ref, oc_ref, g_ref, wb_ref, wo_ref, o_ref):
    merged = None
    for n, br in enumerate((oa_ref, ob_ref, oc_ref)):
        y = jnp.dot(br[...], wb_ref[n], preferred_element_type=F32)
        term = jax.nn.sigmoid(g_ref[:, n * D_MODEL:(n + 1) * D_MODEL]) * y
        merged = term if merged is None else merged + term
    o_ref[...] = x_ref[...] + jnp.dot(merged.astype(BF16), wo_ref[...], preferred_element_type=F32)


def _merge(x2, oa, ob, oc, proj2, w_branch, w_out, tm=512):
    t = x2.shape[0]
    assert t % tm == 0
    row = lambda i: (i, 0)
    return pl.pallas_call(
        _merge_kernel,
        grid=(t // tm,),
        in_specs=[
            pl.BlockSpec((tm, D_MODEL), row),
            pl.BlockSpec((tm, BRANCH_WIDTH), row),
            pl.BlockSpec((tm, BRANCH_WIDTH), row),
            pl.BlockSpec((tm, BRANCH_WIDTH), row),
            pl.BlockSpec((tm, N_BRANCH * D_MODEL), lambda i: (i, COL_GATES // (N_BRANCH * D_MODEL))),
            pl.BlockSpec((N_BRANCH, BRANCH_WIDTH, D_MODEL), lambda i: (0, 0, 0)),
            pl.BlockSpec((D_MODEL, D_MODEL), lambda i: (0, 0)),
        ],
        out_specs=pl.BlockSpec((tm, D_MODEL), row),
        out_shape=jax.ShapeDtypeStruct((t, D_MODEL), F32),
        compiler_params=_cparams(1),
        name="gated_merge_out_proj",
    )(x2, oa, ob, oc, proj2, w_branch, w_out)


def _ffn_kernel(x_ref, g_ref, w1_ref, w2_ref, gf_ref, o_ref, h_ref, acc_ref, *, final_norm):
    j = pl.program_id(1)

    @pl.when(j == 0)
    def _():
        x = x_ref[...]
        ms = jnp.mean(x * x, axis=-1, keepdims=True)
        h_ref[...] = (x * lax.rsqrt(ms + NORM_EPS) * g_ref[...]).astype(BF16)
        acc_ref[...] = jnp.zeros(acc_ref.shape, F32)

    u = jnp.maximum(jnp.dot(h_ref[...], w1_ref[...], preferred_element_type=F32), 0.0)
    acc_ref[...] += jnp.dot((u * u).astype(BF16), w2_ref[...], preferred_element_type=F32)

    @pl.when(j == pl.num_programs(1) - 1)
    def _():
        y = x_ref[...] + acc_ref[...]
        if final_norm:
            ms = jnp.mean(y * y, axis=-1, keepdims=True)
            y = y * lax.rsqrt(ms + NORM_EPS) * gf_ref[...]
        o_ref[...] = y


def _ffn(x2, gain, w1, w2, gain_f, final_norm, tm=512, tf=1024):
    t = x2.shape[0]
    assert t % tm == 0 and D_FF % tf == 0
    kern = functools.partial(_ffn_kernel, final_norm=final_norm)
    return pl.pallas_call(
        kern,
        grid=(t // tm, D_FF // tf),
        in_specs=[
            pl.BlockSpec((tm, D_MODEL), lambda i, j: (i, 0)),
            pl.BlockSpec((1, D_MODEL), lambda i, j: (0, 0)),
            pl.BlockSpec((D_MODEL, tf), lambda i, j: (0, j)),
            pl.BlockSpec((tf, D_MODEL), lambda i, j: (j, 0)),
            pl.BlockSpec((1, D_MODEL), lambda i, j: (0, 0)),
        ],
        out_specs=pl.BlockSpec((tm, D_MODEL), lambda i, j: (i, 0)),
        out_shape=jax.ShapeDtypeStruct((t, D_MODEL), F32),
        scratch_shapes=[pltpu.VMEM((tm, D_MODEL), BF16), pltpu.VMEM((tm, D_MODEL), F32)],
        compiler_params=_cparams(2),
        name="relu2_mlp",
    )(x2, gain, w1, w2, gain_f)


def _pack_w_in(w_in):
    sizes = (A_HEADS * HEAD_DIM, HEAD_DIM, HEAD_DIM, IDX_HEADS * IDX_DIM, IDX_DIM, IDX_HEADS,
             B_HEADS * HEAD_DIM, B_KV_HEADS * HEAD_DIM, B_KV_HEADS * HEAD_DIM,
             C_HEADS * HEAD_DIM, C_HEADS * HEAD_DIM, C_HEADS * HEAD_DIM, N_BRANCH * D_MODEL)
    offs = np.concatenate([[0], np.cumsum(sizes)])
    aq, ak, av, iq, ik, iw, bq, bk, bv, cq, ck, cv, gates = [
        w_in[..., offs[n]:offs[n + 1]] for n in range(len(sizes))]
    lead = w_in.shape[:-1]
    pad_iw = jnp.zeros(lead + (LANES - HEAD_DIM - IDX_HEADS,), w_in.dtype)
    parts = [gates, aq, bq, cq, ck, iq, bk, ak, ik, av, iw, pad_iw, bv, cv]
    used = sum(p.shape[-1] for p in parts)
    parts.append(jnp.zeros(lead + (N_PACKED - used,), w_in.dtype))
    return jnp.concatenate(parts, axis=-1).astype(BF16)


def _rope_tables(seq):
    half = HEAD_DIM // 2
    inv = 1.0 / (ROPE_THETA ** (jnp.arange(0, HEAD_DIM, 2, dtype=F32) / HEAD_DIM))
    ang = jnp.arange(seq, dtype=F32)[:, None] * inv[None, :]
    cos, sin = jnp.cos(ang), jnp.sin(ang)
    reps = LANES // HEAD_DIM
    cos_t = jnp.tile(jnp.concatenate([cos, cos], axis=1), (1, reps))
    sin_t = jnp.tile(jnp.concatenate([-sin, sin], axis=1), (1, reps))
    assert cos_t.shape == (seq, LANES) and half * 2 == HEAD_DIM
    return cos_t, sin_t


def kernel(x, norm1, w_in, sinks, w_branch, w_out, norm2, w_ff1, w_ff2, norm_f):
    b, seq, d = x.shape
    depth = w_in.shape[0]
    cos_t, sin_t = _rope_tables(seq)
    w_in_p = _pack_w_in(w_in)
    w_branch_b = w_branch.astype(BF16)
    w_out_b = w_out.astype(BF16)
    w_ff1_b = w_ff1.astype(BF16)
    w_ff2_b = w_ff2.astype(BF16)
    gain_f = norm_f.reshape(1, d)

    x2 = x.reshape(b * seq, d)
    for l in range(depth):
        proj2 = _project(x2, norm1[l].reshape(1, d), w_in_p[l], cos_t, sin_t, seq)
        proj3 = proj2.reshape(b, seq, N_PACKED)
        o_a = _dsa(proj3).reshape(b * seq, BRANCH_WIDTH)
        o_b = _swa(proj3, sinks[l]).reshape(b * seq, BRANCH_WIDTH)
        o_c = _moba(proj3).reshape(b * seq, BRANCH_WIDTH)
        x2 = _merge(x2, o_a, o_b, o_c, proj2, w_branch_b[l], w_out_b[l])
        x2 = _ffn(x2, norm2[l].reshape(1, d), w_ff1_b[l], w_ff2_b[l], gain_f,
                  final_norm=(l == depth - 1))
    return x2.reshape(b, seq, d)
```

```python
import functools

import jax
import jax.numpy as jnp
import numpy as np
from jax import lax
from jax.experimental import pallas as pl
from jax.experimental.pallas import tpu as pltpu

F32 = jnp.float32
BF16 = jnp.bfloat16
I32 = jnp.int32

D_MODEL = 1024
HEAD_DIM = 64
ROPE_THETA = 10000.0
NORM_EPS = 1e-6
A_HEADS = 8
IDX_HEADS = 4
IDX_DIM = 64
IDX_TOPK_MAX = 256
B_HEADS = 8
B_KV_HEADS = 2
WINDOW = 128
C_HEADS = 8
MOBA_BLOCK = 256
MOBA_TOPK = 3
Q_BLOCK = 128
N_BRANCH = 3
BRANCH_WIDTH = 8 * HEAD_DIM
D_FF = 4 * D_MODEL

LANES = 128
NEG_BIG = -1e30
Q_SCALE_LOG2 = (HEAD_DIM ** -0.5) * float(np.log2(np.e))
INT_MIN = -(2 ** 31)
BISECT_UNCHECKED_STEPS = 14

COL_GATES = 0
COL_AQ = 3072
COL_BQ = 3584
COL_CQ = 4096
COL_CK = 4608
COL_IQ = 5120
COL_BK = 5376
COL_AKIK = 5504
ROPE_BEGIN, ROPE_END = 3072, 5632
COL_AVIW = 5632
COL_BV = 5760
COL_CV = 5888
N_PACKED = 6656

VMEM_LIMIT = 56 * 1024 * 1024


def _cparams(n_axes):
    return pltpu.CompilerParams(dimension_semantics=("arbitrary",) * n_axes,
                                vmem_limit_bytes=VMEM_LIMIT)


def _nt_dot(a, b):
    return lax.dot_general(a, b, (((1,), (1,)), ((), ())), preferred_element_type=F32)


def _pipelined(n, produce, consume, init):
    produce(0, 0)

    def pair(t, carry):
        c = 2 * t
        produce(c + 1, 1)
        carry = consume(c, 0, carry, False)
        produce(c + 2, 0)
        return consume(c + 1, 1, carry, False)
    n_pairs = (n - 1) // 2
    carry = lax.fori_loop(0, n_pairs, pair, init)
    c0 = 2 * n_pairs

    def two_left(carry):
        produce(c0 + 1, 1)
        carry = consume(c0, 0, carry, False)
        return consume(c0 + 1, 1, carry, True)

    def one_left(carry):
        return consume(c0, 0, carry, True)
    return lax.cond(n - c0 == 2, two_left, one_left, carry)


def _split3_lhs(x):
    hi = x.astype(BF16)
    lo = (x - hi.astype(F32)).astype(BF16)
    return jnp.concatenate([hi, hi, lo], axis=1)


def _split3_rhs(x):
    hi = x.astype(BF16)
    lo = (x - hi.astype(F32)).astype(BF16)
    return jnp.concatenate([hi, lo, hi], axis=1)


def _proj_kernel(x_ref, g_ref, w_ref, cos_ref, sin_ref, o_ref, h_ref, *, rope_lo, rope_hi, tn):
    j = pl.program_id(1)

    @pl.when(j == 0)
    def _():
        x = x_ref[...]
        ms = jnp.mean(x * x, axis=-1, keepdims=True)
        h_ref[...] = (x * lax.rsqrt(ms + NORM_EPS) * g_ref[...]).astype(BF16)

    acc = jnp.dot(h_ref[...], w_ref[...], preferred_element_type=F32)
    is_rope = jnp.logical_and(j >= rope_lo, j < rope_hi)

    @pl.when(is_rope)
    def _():
        cos = cos_ref[...]
        sin = sin_ref[...]
        lane = lax.broadcasted_iota(I32, cos.shape, 1)
        first_half = (lane % HEAD_DIM) < (HEAD_DIM // 2)
        for c in range(tn // LANES):
            a = acc[:, c * LANES:(c + 1) * LANES]
            partner = jnp.where(first_half,
                                pltpu.roll(a, LANES - HEAD_DIM // 2, 1),
                                pltpu.roll(a, HEAD_DIM // 2, 1))
            o_ref[:, c * LANES:(c + 1) * LANES] = a * cos + partner * sin

    @pl.when(jnp.logical_not(is_rope))
    def _():
        o_ref[...] = acc


def _project(x2, gain, w_packed, cos_t, sin_t, seq, tm=1024, tn=512):
    t = x2.shape[0]
    tm = min(tm, seq)
    n = w_packed.shape[1]
    assert t % tm == 0 and seq % tm == 0 and n % tn == 0
    assert ROPE_BEGIN % tn == 0 and ROPE_END % tn == 0
    pos_blocks = seq // tm
    kern = functools.partial(_proj_kernel, rope_lo=ROPE_BEGIN // tn, rope_hi=ROPE_END // tn, tn=tn)
    return pl.pallas_call(
        kern,
        grid=(t // tm, n // tn),
        in_specs=[
            pl.BlockSpec((tm, D_MODEL), lambda i, j: (i, 0)),
            pl.BlockSpec((1, D_MODEL), lambda i, j: (0, 0)),
            pl.BlockSpec((D_MODEL, tn), lambda i, j: (0, j)),
            pl.BlockSpec((tm, LANES), lambda i, j: (i % pos_blocks, 0)),
            pl.BlockSpec((tm, LANES), lambda i, j: (i % pos_blocks, 0)),
        ],
        out_specs=pl.BlockSpec((tm, tn), lambda i, j: (i, j)),
        out_shape=jax.ShapeDtypeStruct((t, n), F32),
        scratch_shapes=[pltpu.VMEM((tm, D_MODEL), BF16)],
        compiler_params=_cparams(2),
        name="norm_proj_rope",
    )(x2, gain, w_packed, cos_t, sin_t)


def _value_with_ones(v):
    lane = lax.broadcasted_iota(I32, v.shape, 1)
    return jnp.concatenate([v.astype(BF16), jnp.where(lane == 0, 1.0, 0.0).astype(BF16)], axis=1)


def _dsa_kernel(aq_ref, iq_ref, kk_ref, vw_ref, o_ref,
                kbf_ref, vbf_ref, ikc_ref, keys_ref, qall_ref, mrow_ref, acc_ref, tri_ref, ones_ref,
                sbuf_ref, *, seq, kc, qb, top_k):
    i = pl.program_id(1)
    n_prep = seq // kc
    idx_scale = (IDX_DIM ** -0.5) * (IDX_HEADS ** -0.5)

    @pl.when(i == 0)
    def _prep():
        def body(c, carry):
            rows = pl.ds(pl.multiple_of(c * kc, kc), kc)
            kk = kk_ref[rows, :]
            kbf_ref[rows, :] = kk[:, :HEAD_DIM].astype(BF16)
            ikc_ref[rows, :] = _split3_rhs(kk[:, HEAD_DIM:])
            vbf_ref[rows, :] = _value_with_ones(vw_ref[rows, :HEAD_DIM])
            return carry
        lax.fori_loop(0, n_prep, body, 0)
        r = lax.broadcasted_iota(I32, (kc, kc), 0)
        cidx = lax.broadcasted_iota(I32, (kc, kc), 1)
        tri_ref[...] = jnp.where(r <= cidx, 1.0, 0.0).astype(BF16)
        ones_ref[...] = jnp.ones(ones_ref.shape, BF16)

    nkc = (i * qb) // kc + 1

    iq = iq_ref[...]
    w = vw_ref[pl.ds(pl.multiple_of(i * qb, qb), qb), HEAD_DIM:HEAD_DIM + IDX_HEADS] * idx_scale
    qcat = jnp.concatenate(
        [_split3_lhs(iq[:, h * IDX_DIM:(h + 1) * IDX_DIM]) for h in range(IDX_HEADS)], axis=0)
    row = lax.broadcasted_iota(I32, (qb, LANES), 0)
    lane = lax.broadcasted_iota(I32, (qb, LANES), 1)
    qpos = i * qb + row
    w_b = [jnp.broadcast_to(w[:, h:h + 1], (qb, LANES)) for h in range(IDX_HEADS)]
    ngrp = kc // LANES

    def chunk_rows(c):
        return pl.ds(pl.multiple_of(c * kc, kc), kc)

    def to_key(x):
        bits = lax.bitcast_convert_type(x, I32)
        return jnp.where(bits < 0, bits ^ jnp.int32(0x7FFFFFFF), bits)

    def idx_dots(c, slot):
        sbuf_ref[slot, :IDX_HEADS * qb] = _nt_dot(qcat, ikc_ref[chunk_rows(c), :])

    def score_chunk(c, slot, smax, is_last):
        for g in range(ngrp):
            sc = None
            for h in range(IDX_HEADS):
                d = sbuf_ref[slot, h * qb:(h + 1) * qb, g * LANES:(g + 1) * LANES]
                term = jnp.maximum(d, 0.0) * w_b[h]
                sc = term if sc is None else sc + term
            sc = jnp.where(sc == 0.0, 0.0, sc)
            causal = c * kc + g * LANES + lane <= qpos
            keys_ref[c, :, g * LANES:(g + 1) * LANES] = jnp.where(causal, to_key(sc), jnp.int32(INT_MIN))
            smax = jnp.maximum(smax, jnp.where(causal, sc, -jnp.inf))
        return smax

    smax = _pipelined(nkc, idx_dots, score_chunk, jnp.full((qb, LANES), -jnp.inf, F32))
    key_max = to_key(jnp.max(smax, axis=1, keepdims=True))

    def count(pred, thr):
        parts = []
        for r0 in range(0, qb, LANES):
            thr_b = jnp.broadcast_to(thr[r0:r0 + LANES], (LANES, LANES))

            def body(c, acc, r0=r0, thr_b=thr_b):
                kch = keys_ref[c, r0:r0 + LANES, :]
                for g in range(kc // LANES):
                    acc = acc + jnp.where(pred(kch[:, g * LANES:(g + 1) * LANES], thr_b), 1.0, 0.0)
                return acc
            parts.append(lax.fori_loop(0, nkc, body, jnp.zeros((LANES, LANES), F32)))
        return jnp.sum(jnp.concatenate(parts, axis=0), axis=1, keepdims=True)

    ge = lambda k, t: k >= t
    kf = float(top_k)

    def probe(v, state):
        lo, n_lo, hi = state
        n_v = count(ge, v)
        inside = jnp.logical_and(v > lo, v < hi)
        up = jnp.logical_and(inside, n_v >= kf)
        down = jnp.logical_and(inside, n_v < kf)
        return jnp.where(up, v, lo), jnp.where(up, n_v, n_lo), jnp.where(down, v, hi)

    def bisect(state):
        lo, _, hi = state
        return probe(lo + lax.shift_right_logical(hi - lo, jnp.int32(1)), state)

    def unfinished(state):
        lo, n_lo, hi = state
        done = jnp.logical_or(n_lo == kf, hi - 1 == lo)
        return jnp.max(jnp.where(done, 0.0, 1.0)) > 0.0

    state = (jnp.full((qb, 1), INT_MIN, I32), jnp.full((qb, 1), float(seq + 1), F32), key_max + 1)
    state = probe(jnp.ones((qb, 1), I32), state)
    state = probe(jnp.zeros((qb, 1), I32), state)
    two_binades = jnp.int32(1 << 24)
    state = probe(jnp.maximum(key_max, jnp.int32(INT_MIN) + two_binades) - two_binades, state)
    state = lax.fori_loop(0, BISECT_UNCHECKED_STEPS, lambda _, st: bisect(st), state)

    def bisect_cond(carry):
        return jnp.logical_and(carry[0], carry[1] < 34)

    def bisect_body(carry):
        state = bisect(carry[2:])
        return (unfinished(state), carry[1] + 1) + state
    carry = lax.while_loop(bisect_cond, bisect_body, (unfinished(state), jnp.int32(0)) + state)
    thr = carry[2]

    n_gt = count(lambda k, t: k > t, thr)
    need = jnp.where(thr > jnp.int32(INT_MIN), kf - n_gt, 0.0)
    thr_b = jnp.broadcast_to(thr, (qb, LANES))
    need_b = jnp.broadcast_to(need, (qb, LANES))

    aq = aq_ref[...]
    for h in range(A_HEADS):
        qall_ref[h * qb:(h + 1) * qb, :] = (
            aq[:, h * HEAD_DIM:(h + 1) * HEAD_DIM] * Q_SCALE_LOG2).astype(BF16)

    def logits(c, slot):
        sbuf_ref[slot] = _nt_dot(qall_ref[...], kbf_ref[chunk_rows(c), :])

    def bias_chunk(c, ties_before):
        kch = keys_ref[c]
        eq = [kch[:, g * LANES:(g + 1) * LANES] == thr_b for g in range(ngrp)]
        eq_b = jnp.concatenate([jnp.where(e, 1.0, 0.0).astype(BF16) for e in eq], axis=1)
        rank = jnp.dot(eq_b, tri_ref[...], preferred_element_type=F32)
        bias = jnp.concatenate(
            [jnp.where(jnp.logical_or(
                kch[:, g * LANES:(g + 1) * LANES] > thr_b,
                jnp.logical_and(eq[g], rank[:, g * LANES:(g + 1) * LANES] + ties_before <= need_b)),
                0.0, NEG_BIG) for g in range(ngrp)], axis=1)
        keys_ref[c] = lax.bitcast_convert_type(bias, I32)
        return ties_before + jnp.dot(eq_b, ones_ref[...], preferred_element_type=F32)
    lax.fori_loop(0, nkc, bias_chunk, jnp.zeros((qb, LANES), F32))

    def masked(slot, bias):
        return (sbuf_ref[slot].reshape(A_HEADS, qb, kc) + bias[None]).reshape(A_HEADS * qb, kc)

    def att_chunk(c, slot, carry, is_last):
        s = masked(slot, lax.bitcast_convert_type(keys_ref[c], F32))
        m_old = mrow_ref[...]
        mx = s[:, :LANES]
        for g in range(1, ngrp):
            mx = jnp.maximum(mx, s[:, g * LANES:(g + 1) * LANES])
        m_new = jnp.maximum(m_old, jnp.broadcast_to(jnp.max(mx, axis=1, keepdims=True), m_old.shape))
        p = jnp.concatenate(
            [jnp.exp2(s[:, g * LANES:(g + 1) * LANES] - m_new).astype(BF16) for g in range(ngrp)], axis=1)
        acc_ref[...] = jnp.exp2(m_old - m_new) * acc_ref[...] + jnp.dot(
            p, vbf_ref[chunk_rows(c), :], preferred_element_type=F32)
        mrow_ref[...] = m_new
        return carry
    mrow_ref[...] = jnp.full(mrow_ref.shape, NEG_BIG, F32)
    acc_ref[...] = jnp.zeros(acc_ref.shape, F32)
    _pipelined(nkc, logits, att_chunk, 0)

    acc = acc_ref[...]
    out = acc[:, :HEAD_DIM] / acc[:, HEAD_DIM:HEAD_DIM + 1]
    for h in range(A_HEADS):
        o_ref[:, h * HEAD_DIM:(h + 1) * HEAD_DIM] = out[h * qb:(h + 1) * qb].astype(o_ref.dtype)


def _dsa(proj3, kc=512, qb=256):
    b, seq, _ = proj3.shape
    top_k = min(IDX_TOPK_MAX, seq // 4)
    kc = min(kc, seq)
    assert seq % kc == 0 and kc % qb == 0 and seq % qb == 0 and kc >= top_k
    kern = functools.partial(_dsa_kernel, seq=seq, kc=kc, qb=qb, top_k=top_k)
    hq = A_HEADS * qb
    return pl.pallas_call(
        kern,
        grid=(b, seq // qb),
        in_specs=[
            pl.BlockSpec((None, qb, A_HEADS * HEAD_DIM), lambda bb, i: (bb, i, COL_AQ // 512)),
            pl.BlockSpec((None, qb, IDX_HEADS * IDX_DIM), lambda bb, i: (bb, i, COL_IQ // 256)),
            pl.BlockSpec((None, seq, LANES), lambda bb, i: (bb, 0, COL_AKIK // LANES),
                         pipeline_mode=pl.Buffered(1)),
            pl.BlockSpec((None, seq, LANES), lambda bb, i: (bb, 0, COL_AVIW // LANES),
                         pipeline_mode=pl.Buffered(1)),
        ],
        out_specs=pl.BlockSpec((None, qb, A_HEADS * HEAD_DIM), lambda bb, i: (bb, i, 0)),
        out_shape=jax.ShapeDtypeStruct((b, seq, A_HEADS * HEAD_DIM), BF16),
        scratch_shapes=[
            pltpu.VMEM((seq, HEAD_DIM), BF16),
            pltpu.VMEM((seq, LANES), BF16),
            pltpu.VMEM((seq, 3 * IDX_DIM), BF16),
            pltpu.VMEM((seq // kc, qb, kc), I32),
            pltpu.VMEM((hq, HEAD_DIM), BF16),
            pltpu.VMEM((hq, LANES), F32),
            pltpu.VMEM((hq, LANES), F32),
            pltpu.VMEM((kc, kc), BF16),
            pltpu.VMEM((kc, LANES), BF16),
            pltpu.VMEM((2, hq, kc), F32),
        ],
        compiler_params=_cparams(2),
        name="dsa_topk_attention",
    )(proj3, proj3, proj3, proj3)


def _swa_kernel(sink_ref, q_ref, kp_ref, kc_ref, vp_ref, vc_ref, o_ref):
    i = pl.program_id(1)
    w = WINDOW
    group = B_HEADS // B_KV_HEADS
    q = q_ref[...]
    k = jnp.concatenate([kp_ref[...], kc_ref[...]], axis=0)
    v = jnp.concatenate([vp_ref[...], vc_ref[...]], axis=0)
    row = lax.broadcasted_iota(I32, (w, 2 * w), 0)
    col = lax.broadcasted_iota(I32, (w, 2 * w), 1)
    kpos = (i - 1) * w + col
    diff = i * w + row - kpos
    mask = jnp.logical_and(jnp.logical_and(diff >= 0, diff < w), kpos >= 0)
    for g in range(B_KV_HEADS):
        kg = k[:, g * HEAD_DIM:(g + 1) * HEAD_DIM].astype(BF16)
        vg = v[:, g * HEAD_DIM:(g + 1) * HEAD_DIM].astype(BF16)
        for hh in range(group):
            h = g * group + hh
            qh = (q[:, h * HEAD_DIM:(h + 1) * HEAD_DIM] * (HEAD_DIM ** -0.5)).astype(BF16)
            s = jnp.where(mask, _nt_dot(qh, kg), NEG_BIG)
            sink = sink_ref[h]
            m = jnp.maximum(jnp.max(s, axis=1, keepdims=True), sink)
            p = jnp.exp(s - m)
            den = jnp.sum(p, axis=1, keepdims=True) + jnp.exp(sink - m)
            o = jnp.dot(p.astype(BF16), vg, preferred_element_type=F32) / den
            o_ref[:, h * HEAD_DIM:(h + 1) * HEAD_DIM] = o.astype(o_ref.dtype)


def _swa(proj3, sinks):
    b, seq, _ = proj3.shape
    w = WINDOW
    kvw = B_KV_HEADS * HEAD_DIM
    assert kvw == LANES and seq % w == 0
    prev = lambda col: (lambda bb, i: (bb, jnp.maximum(i - 1, 0), col))
    cur = lambda col: (lambda bb, i: (bb, i, col))
    return pl.pallas_call(
        _swa_kernel,
        grid=(b, seq // w),
        in_specs=[
            pl.BlockSpec(memory_space=pltpu.SMEM),
            pl.BlockSpec((None, w, B_HEADS * HEAD_DIM), cur(COL_BQ // 512)),
            pl.BlockSpec((None, w, kvw), prev(COL_BK // LANES)),
            pl.BlockSpec((None, w, kvw), cur(COL_BK // LANES)),
            pl.BlockSpec((None, w, kvw), prev(COL_BV // LANES)),
            pl.BlockSpec((None, w, kvw), cur(COL_BV // LANES)),
        ],
        out_specs=pl.BlockSpec((None, w, B_HEADS * HEAD_DIM), lambda bb, i: (bb, i, 0)),
        out_shape=jax.ShapeDtypeStruct((b, seq, B_HEADS * HEAD_DIM), BF16),
        compiler_params=_cparams(2),
        name="swa_sink_attention",
    )(sinks, proj3, proj3, proj3, proj3, proj3)


def _moba_kernel(q_ref, k_ref, v_ref, o_ref,
                 kaug_ref, vaug_ref, kmean_ref, kmw_ref, qaug_ref, mrow_ref, acc_ref, sbuf_ref,
                 *, seq, tq, n_sel):
    i = pl.program_id(2)
    blk = MOBA_BLOCK
    nkb = seq // blk
    heads = LANES // HEAD_DIM
    assert heads == 2
    ones_lane = (HEAD_DIM, 0)

    @pl.when(i == 0)
    def _prep():
        kmean_ref[...] = jnp.zeros(kmean_ref.shape, F32)
        lane_b = lax.broadcasted_iota(I32, (blk, LANES), 1)

        def body(n, carry):
            rows = pl.ds(pl.multiple_of(n * blk, blk), blk)
            kb = k_ref[rows, :]
            vb = v_ref[rows, :]
            onehot = jnp.where(lane_b == n, 1.0, 0.0).astype(BF16)
            for e in range(heads):
                mine = (lane_b // HEAD_DIM) == e
                kaug_ref[e, rows, :] = jnp.concatenate(
                    [onehot, jnp.where(mine, kb, 0.0).astype(BF16)], axis=1)
                ones = jnp.where(lane_b == ones_lane[e], 1.0, 0.0)
                vaug_ref[e, rows, :] = jnp.where(mine, vb, ones).astype(BF16)
            kmean_ref[pl.ds(n, 1), :] = jnp.sum(kb, axis=0, keepdims=True) * (1.0 / blk)
            return carry
        lax.fori_loop(0, nkb, body, 0)
        km = kmean_ref[...]
        lane_m = lax.broadcasted_iota(I32, km.shape, 1)
        kmw = jnp.concatenate(
            [jnp.where((lane_m // HEAD_DIM) == e, km, 0.0) for e in range(heads)], axis=0)
        hi = kmw.astype(BF16)
        kmw_ref[0] = hi
        kmw_ref[1] = (kmw - hi.astype(F32)).astype(BF16)

    q = q_ref[...]
    own1 = (i * tq + lax.broadcasted_iota(I32, (tq, 1), 0)) // blk
    lane = lax.broadcasted_iota(I32, (tq, LANES), 1)
    lanef = lane.astype(F32)

    qhi = q.astype(BF16)
    qlo = (q - qhi.astype(F32)).astype(BF16)
    gates = (_nt_dot(qhi, kmw_ref[0]) + _nt_dot(qhi, kmw_ref[1])) + _nt_dot(qlo, kmw_ref[0])
    q2 = (q * Q_SCALE_LOG2).astype(BF16)
    for e in range(heads):
        g = jnp.where(lane < own1, gates[:, e * LANES:(e + 1) * LANES], -jnp.inf)
        allowed = lane == own1
        for _ in range(n_sel):
            gmax = jnp.max(g, axis=1, keepdims=True)
            is_max = jnp.logical_and(g == gmax, gmax > -jnp.inf)
            first = jnp.min(jnp.where(is_max, lanef, float(LANES)), axis=1, keepdims=True)
            onehot = lanef == first
            allowed = jnp.logical_or(allowed, onehot)
            g = jnp.where(onehot, -jnp.inf, g)
        bias = jnp.where(allowed, 0.0, NEG_BIG).astype(BF16)
        qaug_ref[e] = jnp.concatenate([bias, q2], axis=1)

    kc = tq
    ngrp = kc // LANES
    col = lax.broadcasted_iota(I32, (tq, kc), 1)
    row = lax.broadcasted_iota(I32, (tq, kc), 0)

    def chunk_rows(c):
        return pl.ds(pl.multiple_of(c * kc, kc), kc)

    def logits(c, slot):
        for e in range(heads):
            sbuf_ref[slot, e] = _nt_dot(qaug_ref[e], kaug_ref[e, chunk_rows(c), :])

    def masked(e, slot, diagonal):
        s = sbuf_ref[slot, e]
        return jnp.where(col <= row, s, NEG_BIG) if diagonal else s

    def att_chunk(c, slot, carry, is_last):
        for e in range(heads):
            s = masked(e, slot, is_last)
            m_old = mrow_ref[e]
            mx = s[:, :LANES]
            for gi in range(1, ngrp):
                mx = jnp.maximum(mx, s[:, gi * LANES:(gi + 1) * LANES])
            m_new = jnp.maximum(m_old, jnp.broadcast_to(jnp.max(mx, axis=1, keepdims=True), (tq, LANES)))
            p = jnp.concatenate(
                [jnp.exp2(s[:, gi * LANES:(gi + 1) * LANES] - m_new).astype(BF16) for gi in range(ngrp)],
                axis=1)
            acc_ref[e] = jnp.exp2(m_old - m_new) * acc_ref[e] + jnp.dot(
                p, vaug_ref[e, chunk_rows(c), :], preferred_element_type=F32)
            mrow_ref[e] = m_new
        return carry
    mrow_ref[...] = jnp.full(mrow_ref.shape, NEG_BIG, F32)
    acc_ref[...] = jnp.zeros(acc_ref.shape, F32)
    _pipelined(i + 1, logits, att_chunk, 0)

    outs = []
    for e in range(heads):
        acc = acc_ref[e]
        outs.append(acc / acc[:, ones_lane[e]:ones_lane[e] + 1])
    o_ref[...] = jnp.where(lane < HEAD_DIM, outs[0], outs[1]).astype(o_ref.dtype)


def _moba(proj3, tq=512):
    b, seq, _ = proj3.shape
    tq = min(tq, seq)
    blk = MOBA_BLOCK
    assert seq % blk == 0 and tq % blk == 0 and seq % tq == 0
    nkb = seq // blk
    assert nkb <= LANES
    n_sel = min(MOBA_TOPK, nkb - 1)
    heads = LANES // HEAD_DIM
    kern = functools.partial(_moba_kernel, seq=seq, tq=tq, n_sel=n_sel)
    return pl.pallas_call(
        kern,
        grid=(b, C_HEADS // heads, seq // tq),
        in_specs=[
            pl.BlockSpec((None, tq, LANES), lambda bb, hp, i: (bb, i, COL_CQ // LANES + hp)),
            pl.BlockSpec((None, seq, LANES), lambda bb, hp, i: (bb, 0, COL_CK // LANES + hp)),
            pl.BlockSpec((None, seq, LANES), lambda bb, hp, i: (bb, 0, COL_CV // LANES + hp)),
        ],
        out_specs=pl.BlockSpec((None, tq, LANES), lambda bb, hp, i: (bb, i, hp)),
        out_shape=jax.ShapeDtypeStruct((b, seq, C_HEADS * HEAD_DIM), BF16),
        scratch_shapes=[
            pltpu.VMEM((heads, seq, 2 * LANES), BF16),
            pltpu.VMEM((heads, seq, LANES), BF16),
            pltpu.VMEM((LANES, LANES), F32),
            pltpu.VMEM((2, heads * LANES, LANES), BF16),
            pltpu.VMEM((heads, tq, 2 * LANES), BF16),
            pltpu.VMEM((heads, tq, LANES), F32),
            pltpu.VMEM((heads, tq, LANES), F32),
            pltpu.VMEM((2, heads, tq, tq), F32),
        ],
        compiler_params=_cparams(3),
        name="moba_attention",
    )(proj3, proj3, proj3)


def _merge_kernel(x_ref, oa_ref, ob_ref, oc_ref, g_ref, wb_ref, wo_ref, o_ref):
    merged = None
    for n, br in enumerate((oa_ref, ob_ref, oc_ref)):
        y = jnp.dot(br[...], wb_ref[n], preferred_element_type=F32)
        term = jax.nn.sigmoid(g_ref[:, n * D_MODEL:(n + 1) * D_MODEL]) * y
        merged = term if merged is None else merged + term
    o_ref[...] = x_ref[...] + jnp.dot(merged.astype(BF16), wo_ref[...], preferred_element_type=F32)


def _merge(x2, oa, ob, oc, proj2, w_branch, w_out, tm=512):
    t = x2.shape[0]
    assert t % tm == 0
    row = lambda i: (i, 0)
    return pl.pallas_call(
        _merge_kernel,
        grid=(t // tm,),
        in_specs=[
            pl.BlockSpec((tm, D_MODEL), row),
            pl.BlockSpec((tm, BRANCH_WIDTH), row),
            pl.BlockSpec((tm, BRANCH_WIDTH), row),
            pl.BlockSpec((tm, BRANCH_WIDTH), row),
            pl.BlockSpec((tm, N_BRANCH * D_MODEL), lambda i: (i, COL_GATES // (N_BRANCH * D_MODEL))),
            pl.BlockSpec((N_BRANCH, BRANCH_WIDTH, D_MODEL), lambda i: (0, 0, 0)),
            pl.BlockSpec((D_MODEL, D_MODEL), lambda i: (0, 0)),
        ],
        out_specs=pl.BlockSpec((tm, D_MODEL), row),
        out_shape=jax.ShapeDtypeStruct((t, D_MODEL), F32),
        compiler_params=_cparams(1),
        name="gated_merge_out_proj",
    )(x2, oa, ob, oc, proj2, w_branch, w_out)


def _ffn_kernel(x_ref, g_ref, w1_ref, w2_ref, gf_ref, o_ref, h_ref, acc_ref, *, final_norm):
    j = pl.program_id(1)

    @pl.when(j == 0)
    def _():
        x = x_ref[...]
        ms = jnp.mean(x * x, axis=-1, keepdims=True)
        h_ref[...] = (x * lax.rsqrt(ms + NORM_EPS) * g_ref[...]).astype(BF16)
        acc_ref[...] = jnp.zeros(acc_ref.shape, F32)

    u = jnp.maximum(jnp.dot(h_ref[...], w1_ref[...], preferred_element_type=F32), 0.0)
    acc_ref[...] += jnp.dot((u * u).astype(BF16), w2_ref[...], preferred_element_type=F32)

    @pl.when(j == pl.num_programs(1) - 1)
    def _():
        y = x_ref[...] + acc_ref[...]
        if final_norm:
            ms = jnp.mean(y * y, axis=-1, keepdims=True)
            y = y * lax.rsqrt(ms + NORM_EPS) * gf_ref[...]
        o_ref[...] = y


def _ffn(x2, gain, w1, w2, gain_f, final_norm, tm=512, tf=1024):
    t = x2.shape[0]
    assert t % tm == 0 and D_FF % tf == 0
    kern = functools.partial(_ffn_kernel, final_norm=final_norm)
    return pl.pallas_call(
        kern,
        grid=(t // tm, D_FF // tf),
        in_specs=[
            pl.BlockSpec((tm, D_MODEL), lambda i, j: (i, 0)),
            pl.BlockSpec((1, D_MODEL), lambda i, j: (0, 0)),
            pl.BlockSpec((D_MODEL, tf), lambda i, j: (0, j)),
            pl.BlockSpec((tf, D_MODEL), lambda i, j: (j, 0)),
            pl.BlockSpec((1, D_MODEL), lambda i, j: (0, 0)),
        ],
        out_specs=pl.BlockSpec((tm, D_MODEL), lambda i, j: (i, 0)),
        out_shape=jax.ShapeDtypeStruct((t, D_MODEL), F32),
        scratch_shapes=[pltpu.VMEM((tm, D_MODEL), BF16), pltpu.VMEM((tm, D_MODEL), F32)],
        compiler_params=_cparams(2),
        name="relu2_mlp",
    )(x2, gain, w1, w2, gain_f)


def _pack_w_in(w_in):
    sizes = (A_HEADS * HEAD_DIM, HEAD_DIM, HEAD_DIM, IDX_HEADS * IDX_DIM, IDX_DIM, IDX_HEADS,
             B_HEADS * HEAD_DIM, B_KV_HEADS * HEAD_DIM, B_KV_HEADS * HEAD_DIM,
             C_HEADS * HEAD_DIM, C_HEADS * HEAD_DIM, C_HEADS * HEAD_DIM, N_BRANCH * D_MODEL)
    offs = np.concatenate([[0], np.cumsum(sizes)])
    aq, ak, av, iq, ik, iw, bq, bk, bv, cq, ck, cv, gates = [
        w_in[..., offs[n]:offs[n + 1]] for n in range(len(sizes))]
    lead = w_in.shape[:-1]
    pad_iw = jnp.zeros(lead + (LANES - HEAD_DIM - IDX_HEADS,), w_in.dtype)
    parts = [gates, aq, bq, cq, ck, iq, bk, ak, ik, av, iw, pad_iw, bv, cv]
    used = sum(p.shape[-1] for p in parts)
    parts.append(jnp.zeros(lead + (N_PACKED - used,), w_in.dtype))
    return jnp.concatenate(parts, axis=-1).astype(BF16)


def _rope_tables(seq):
    half = HEAD_DIM // 2
    inv = 1.0 / (ROPE_THETA ** (jnp.arange(0, HEAD_DIM, 2, dtype=F32) / HEAD_DIM))
    ang = jnp.arange(seq, dtype=F32)[:, None] * inv[None, :]
    cos, sin = jnp.cos(ang), jnp.sin(ang)
    reps = LANES // HEAD_DIM
    cos_t = jnp.tile(jnp.concatenate([cos, cos], axis=1), (1, reps))
    sin_t = jnp.tile(jnp.concatenate([-sin, sin], axis=1), (1, reps))
    assert cos_t.shape == (seq, LANES) and half * 2 == HEAD_DIM
    return cos_t, sin_t


def kernel(x, norm1, w_in, sinks, w_branch, w_out, norm2, w_ff1, w_ff2, norm_f):
    b, seq, d = x.shape
    depth = w_in.shape[0]
    cos_t, sin_t = _rope_tables(seq)
    w_in_p = _pack_w_in(w_in)
    w_branch_b = w_branch.astype(BF16)
    w_out_b = w_out.astype(BF16)
    w_ff1_b = w_ff1.astype(BF16)
    w_ff2_b = w_ff2.astype(BF16)
    gain_f = norm_f.reshape(1, d)

    x2 = x.reshape(b * seq, d)
    for l in range(depth):
        proj2 = _project(x2, norm1[l].reshape(1, d), w_in_p[l], cos_t, sin_t, seq)
        proj3 = proj2.reshape(b, seq, N_PACKED)
        o_a = _dsa(proj3).reshape(b * seq, BRANCH_WIDTH)
        o_b = _swa(proj3, sinks[l]).reshape(b * seq, BRANCH_WIDTH)
        o_c = _moba(proj3).reshape(b * seq, BRANCH_WIDTH)
        x2 = _merge(x2, o_a, o_b, o_c, proj2, w_branch_b[l], w_out_b[l])
        x2 = _ffn(x2, norm2[l].reshape(1, d), w_ff1_b[l], w_ff2_b[l], gain_f,
                  final_norm=(l == depth - 1))
    return x2.reshape(b, seq, d)
```

```python
import functools

import jax
import jax.numpy as jnp
import numpy as np
from jax import lax
from jax.experimental import pallas as pl
from jax.experimental.pallas import tpu as pltpu

F32 = jnp.float32
BF16 = jnp.bfloat16
I32 = jnp.int32

D_MODEL = 1024
HEAD_DIM = 64
ROPE_THETA = 10000.0
NORM_EPS = 1e-6
A_HEADS = 8
IDX_HEADS = 4
IDX_DIM = 64
IDX_TOPK_MAX = 256
B_HEADS = 8
B_KV_HEADS = 2
WINDOW = 128
C_HEADS = 8
MOBA_BLOCK = 256
MOBA_TOPK = 3
Q_BLOCK = 128
N_BRANCH = 3
BRANCH_WIDTH = 8 * HEAD_DIM
D_FF = 4 * D_MODEL

LANES = 128
NEG_BIG = -1e30
Q_SCALE_LOG2 = (HEAD_DIM ** -0.5) * float(np.log2(np.e))
INT_MIN = -(2 ** 31)
BISECT_UNCHECKED_STEPS = 14

COL_GATES = 0
COL_AQ = 3072
COL_BQ = 3584
COL_CQ = 4096
COL_CK = 4608
COL_IQ = 5120
COL_BK = 5376
COL_AKIK = 5504
ROPE_BEGIN, ROPE_END = 3072, 5632
COL_AVIW = 5632
COL_BV = 5760
COL_CV = 5888
N_PACKED = 6656

VMEM_LIMIT = 56 * 1024 * 1024


def _cparams(n_axes):
    return pltpu.CompilerParams(dimension_semantics=("arbitrary",) * n_axes,
                                vmem_limit_bytes=VMEM_LIMIT)


def _nt_dot(a, b):
    return lax.dot_general(a, b, (((1,), (1,)), ((), ())), preferred_element_type=F32)


def _pipelined(n, produce, consume, init):
    produce(0, 0)

    def pair(t, carry):
        c = 2 * t
        produce(c + 1, 1)
        carry = consume(c, 0, carry, False)
        produce(c + 2, 0)
        return consume(c + 1, 1, carry, False)
    n_pairs = (n - 1) // 2
    carry = lax.fori_loop(0, n_pairs, pair, init)
    c0 = 2 * n_pairs

    def two_left(carry):
        produce(c0 + 1, 1)
        carry = consume(c0, 0, carry, False)
        return consume(c0 + 1, 1, carry, True)

    def one_left(carry):
        return consume(c0, 0, carry, True)
    return lax.cond(n - c0 == 2, two_left, one_left, carry)


def _split3_lhs(x):
    hi = x.astype(BF16)
    lo = (x - hi.astype(F32)).astype(BF16)
    return jnp.concatenate([hi, hi, lo], axis=1)


def _split3_rhs(x):
    hi = x.astype(BF16)
    lo = (x - hi.astype(F32)).astype(BF16)
    return jnp.concatenate([hi, lo, hi], axis=1)


def _proj_kernel(x_ref, g_ref, w_ref, cos_ref, sin_ref, o_ref, h_ref, *, rope_lo, rope_hi, tn):
    j = pl.program_id(1)

    @pl.when(j == 0)
    def _():
        x = x_ref[...]
        ms = jnp.mean(x * x, axis=-1, keepdims=True)
        h_ref[...] = (x * lax.rsqrt(ms + NORM_EPS) * g_ref[...]).astype(BF16)

    acc = jnp.dot(h_ref[...], w_ref[...], preferred_element_type=F32)
    is_rope = jnp.logical_and(j >= rope_lo, j < rope_hi)

    @pl.when(is_rope)
    def _():
        cos = cos_ref[...]
        sin = sin_ref[...]
        lane = lax.broadcasted_iota(I32, cos.shape, 1)
        first_half = (lane % HEAD_DIM) < (HEAD_DIM // 2)
        for c in range(tn // LANES):
            a = acc[:, c * LANES:(c + 1) * LANES]
            partner = jnp.where(first_half,
                                pltpu.roll(a, LANES - HEAD_DIM // 2, 1),
                                pltpu.roll(a, HEAD_DIM // 2, 1))
            o_ref[:, c * LANES:(c + 1) * LANES] = a * cos + partner * sin

    @pl.when(jnp.logical_not(is_rope))
    def _():
        o_ref[...] = acc


def _project(x2, gain, w_packed, cos_t, sin_t, seq, tm=1024, tn=512):
    t = x2.shape[0]
    tm = min(tm, seq)
    n = w_packed.shape[1]
    assert t % tm == 0 and seq % tm == 0 and n % tn == 0
    assert ROPE_BEGIN % tn == 0 and ROPE_END % tn == 0
    pos_blocks = seq // tm
    kern = functools.partial(_proj_kernel, rope_lo=ROPE_BEGIN // tn, rope_hi=ROPE_END // tn, tn=tn)
    return pl.pallas_call(
        kern,
        grid=(t // tm, n // tn),
        in_specs=[
            pl.BlockSpec((tm, D_MODEL), lambda i, j: (i, 0)),
            pl.BlockSpec((1, D_MODEL), lambda i, j: (0, 0)),
            pl.BlockSpec((D_MODEL, tn), lambda i, j: (0, j)),
            pl.BlockSpec((tm, LANES), lambda i, j: (i % pos_blocks, 0)),
            pl.BlockSpec((tm, LANES), lambda i, j: (i % pos_blocks, 0)),
        ],
        out_specs=pl.BlockSpec((tm, tn), lambda i, j: (i, j)),
        out_shape=jax.ShapeDtypeStruct((t, n), F32),
        scratch_shapes=[pltpu.VMEM((tm, D_MODEL), BF16)],
        compiler_params=_cparams(2),
        name="norm_proj_rope",
    )(x2, gain, w_packed, cos_t, sin_t)


def _value_with_ones(v):
    lane = lax.broadcasted_iota(I32, v.shape, 1)
    return jnp.concatenate([v.astype(BF16), jnp.where(lane == 0, 1.0, 0.0).astype(BF16)], axis=1)


def _dsa_kernel(aq_ref, iq_ref, kk_ref, vw_ref, o_ref,
                kbf_ref, vbf_ref, ikc_ref, keys_ref, qall_ref, mrow_ref, acc_ref, tri_ref, ones_ref,
                sbuf_ref, *, seq, kc, qb, top_k):
    i = pl.program_id(1)
    n_prep = seq // kc
    idx_scale = (IDX_DIM ** -0.5) * (IDX_HEADS ** -0.5)

    @pl.when(i == 0)
    def _prep():
        def body(c, carry):
            rows = pl.ds(pl.multiple_of(c * kc, kc), kc)
            kk = kk_ref[rows, :]
            kbf_ref[rows, :] = kk[:, :HEAD_DIM].astype(BF16)
            ikc_ref[rows, :] = _split3_rhs(kk[:, HEAD_DIM:])
            vbf_ref[rows, :] = _value_with_ones(vw_ref[rows, :HEAD_DIM])
            return carry
        lax.fori_loop(0, n_prep, body, 0)
        r = lax.broadcasted_iota(I32, (kc, kc), 0)
        cidx = lax.broadcasted_iota(I32, (kc, kc), 1)
        tri_ref[...] = jnp.where(r <= cidx, 1.0, 0.0).astype(BF16)
        ones_ref[...] = jnp.ones(ones_ref.shape, BF16)

    nkc = (i * qb) // kc + 1

    iq = iq_ref[...]
    w = vw_ref[pl.ds(pl.multiple_of(i * qb, qb), qb), HEAD_DIM:HEAD_DIM + IDX_HEADS] * idx_scale
    qcat = jnp.concatenate(
        [_split3_lhs(iq[:, h * IDX_DIM:(h + 1) * IDX_DIM]) for h in range(IDX_HEADS)], axis=0)
    row = lax.broadcasted_iota(I32, (qb, LANES), 0)
    lane = lax.broadcasted_iota(I32, (qb, LANES), 1)
    qpos = i * qb + row
    w_b = [jnp.broadcast_to(w[:, h:h + 1], (qb, LANES)) for h in range(IDX_HEADS)]
    ngrp = kc // LANES

    def chunk_rows(c):
        return pl.ds(pl.multiple_of(c * kc, kc), kc)

    def to_key(x):
        bits = lax.bitcast_convert_type(x, I32)
        return jnp.where(bits < 0, bits ^ jnp.int32(0x7FFFFFFF), bits)

    def idx_dots(c, slot):
        sbuf_ref[slot, :IDX_HEADS * qb] = _nt_dot(qcat, ikc_ref[chunk_rows(c), :])

    def score_chunk(c, slot, smax, is_last):
        for g in range(ngrp):
            sc = None
            for h in range(IDX_HEADS):
                d = sbuf_ref[slot, h * qb:(h + 1) * qb, g * LANES:(g + 1) * LANES]
                term = jnp.maximum(d, 0.0) * w_b[h]
                sc = term if sc is None else sc + term
            sc = jnp.where(sc == 0.0, 0.0, sc)
            key = to_key(sc)
            if is_last:
                causal = c * kc + g * LANES + lane <= qpos
                key = jnp.where(causal, key, jnp.int32(INT_MIN))
                sc = jnp.where(causal, sc, -jnp.inf)
            keys_ref[c, :, g * LANES:(g + 1) * LANES] = key
            smax = jnp.maximum(smax, sc)
        return smax

    smax = _pipelined(nkc, idx_dots, score_chunk, jnp.full((qb, LANES), -jnp.inf, F32))
    key_max = to_key(jnp.max(smax, axis=1, keepdims=True))

    def count(pred, thr):
        parts = []
        for r0 in range(0, qb, LANES):
            thr_b = jnp.broadcast_to(thr[r0:r0 + LANES], (LANES, LANES))

            def body(c, acc, r0=r0, thr_b=thr_b):
                kch = keys_ref[c, r0:r0 + LANES, :]
                for g in range(kc // LANES):
                    acc = acc + jnp.where(pred(kch[:, g * LANES:(g + 1) * LANES], thr_b), 1.0, 0.0)
                return acc
            parts.append(lax.fori_loop(0, nkc, body, jnp.zeros((LANES, LANES), F32)))
        return jnp.sum(jnp.concatenate(parts, axis=0), axis=1, keepdims=True)

    ge = lambda k, t: k >= t
    kf = float(top_k)

    def probe(v, state):
        lo, n_lo, hi = state
        n_v = count(ge, v)
        inside = jnp.logical_and(v > lo, v < hi)
        up = jnp.logical_and(inside, n_v >= kf)
        down = jnp.logical_and(inside, n_v < kf)
        return jnp.where(up, v, lo), jnp.where(up, n_v, n_lo), jnp.where(down, v, hi)

    def bisect(state):
        lo, _, hi = state
        return probe(lo + lax.shift_right_logical(hi - lo, jnp.int32(1)), state)

    def unfinished(state):
        lo, n_lo, hi = state
        done = jnp.logical_or(n_lo == kf, hi - 1 == lo)
        return jnp.max(jnp.where(done, 0.0, 1.0)) > 0.0

    state = (jnp.full((qb, 1), INT_MIN, I32), jnp.full((qb, 1), float(seq + 1), F32), key_max + 1)
    state = probe(jnp.ones((qb, 1), I32), state)
    state = probe(jnp.zeros((qb, 1), I32), state)
    two_binades = jnp.int32(1 << 24)
    state = probe(jnp.maximum(key_max, jnp.int32(INT_MIN) + two_binades) - two_binades, state)
    state = lax.fori_loop(0, BISECT_UNCHECKED_STEPS, lambda _, st: bisect(st), state)

    def bisect_cond(carry):
        return jnp.logical_and(carry[0], carry[1] < 34)

    def bisect_body(carry):
        state = bisect(carry[2:])
        return (unfinished(state), carry[1] + 1) + state
    carry = lax.while_loop(bisect_cond, bisect_body, (unfinished(state), jnp.int32(0)) + state)
    thr = carry[2]

    n_gt = count(lambda k, t: k > t, thr)
    need = jnp.where(thr > jnp.int32(INT_MIN), kf - n_gt, 0.0)
    thr_b = jnp.broadcast_to(thr, (qb, LANES))
    need_b = jnp.broadcast_to(need, (qb, LANES))

    aq = aq_ref[...]
    for h in range(A_HEADS):
        qall_ref[h * qb:(h + 1) * qb, :] = (
            aq[:, h * HEAD_DIM:(h + 1) * HEAD_DIM] * Q_SCALE_LOG2).astype(BF16)

    def logits(c, slot):
        sbuf_ref[slot] = _nt_dot(qall_ref[...], kbf_ref[chunk_rows(c), :])

    def tie_ranks(c, slot):
        kch = keys_ref[c]
        eq_b = jnp.concatenate(
            [jnp.where(kch[:, g * LANES:(g + 1) * LANES] == thr_b, 1.0, 0.0).astype(BF16)
             for g in range(ngrp)], axis=1)
        sbuf_ref[slot, :qb] = jnp.dot(eq_b, tri_ref[...], preferred_element_type=F32)
        sbuf_ref[slot, qb:2 * qb, :LANES] = jnp.dot(eq_b, ones_ref[...], preferred_element_type=F32)

    def bias_chunk(c, slot, ties_before, is_last):
        kch = keys_ref[c]
        groups = []
        for g in range(ngrp):
            k_g = kch[:, g * LANES:(g + 1) * LANES]
            rank = sbuf_ref[slot, :qb, g * LANES:(g + 1) * LANES] + ties_before
            keep = jnp.logical_or(k_g > thr_b, jnp.logical_and(k_g == thr_b, rank <= need_b))
            groups.append(jnp.where(keep, 0.0, NEG_BIG))
        keys_ref[c] = lax.bitcast_convert_type(jnp.concatenate(groups, axis=1), I32)
        return ties_before + sbuf_ref[slot, qb:2 * qb, :LANES]
    _pipelined(nkc, tie_ranks, bias_chunk, jnp.zeros((qb, LANES), F32))

    def masked(slot, bias):
        return (sbuf_ref[slot].reshape(A_HEADS, qb, kc) + bias[None]).reshape(A_HEADS * qb, kc)

    def att_chunk(c, slot, carry, is_last):
        s = masked(slot, lax.bitcast_convert_type(keys_ref[c], F32))
        m_old = mrow_ref[...]
        mx = s[:, :LANES]
        for g in range(1, ngrp):
            mx = jnp.maximum(mx, s[:, g * LANES:(g + 1) * LANES])
        m_new = jnp.maximum(m_old, jnp.broadcast_to(jnp.max(mx, axis=1, keepdims=True), m_old.shape))
        p = jnp.concatenate(
            [jnp.exp2(s[:, g * LANES:(g + 1) * LANES] - m_new).astype(BF16) for g in range(ngrp)], axis=1)
        acc_ref[...] = jnp.exp2(m_old - m_new) * acc_ref[...] + jnp.dot(
            p, vbf_ref[chunk_rows(c), :], preferred_element_type=F32)
        mrow_ref[...] = m_new
        return carry
    mrow_ref[...] = jnp.full(mrow_ref.shape, NEG_BIG, F32)
    acc_ref[...] = jnp.zeros(acc_ref.shape, F32)
    _pipelined(nkc, logits, att_chunk, 0)

    acc = acc_ref[...]
    out = acc[:, :HEAD_DIM] / acc[:, HEAD_DIM:HEAD_DIM + 1]
    for h in range(A_HEADS):
        o_ref[:, h * HEAD_DIM:(h + 1) * HEAD_DIM] = out[h * qb:(h + 1) * qb].astype(o_ref.dtype)


def _dsa(proj3, kc=512, qb=256):
    b, seq, _ = proj3.shape
    top_k = min(IDX_TOPK_MAX, seq // 4)
    kc = min(kc, seq)
    assert seq % kc == 0 and kc % qb == 0 and seq % qb == 0 and kc >= top_k
    kern = functools.partial(_dsa_kernel, seq=seq, kc=kc, qb=qb, top_k=top_k)
    hq = A_HEADS * qb
    return pl.pallas_call(
        kern,
        grid=(b, seq // qb),
        in_specs=[
            pl.BlockSpec((None, qb, A_HEADS * HEAD_DIM), lambda bb, i: (bb, i, COL_AQ // 512)),
            pl.BlockSpec((None, qb, IDX_HEADS * IDX_DIM), lambda bb, i: (bb, i, COL_IQ // 256)),
            pl.BlockSpec((None, seq, LANES), lambda bb, i: (bb, 0, COL_AKIK // LANES),
                         pipeline_mode=pl.Buffered(1)),
            pl.BlockSpec((None, seq, LANES), lambda bb, i: (bb, 0, COL_AVIW // LANES),
                         pipeline_mode=pl.Buffered(1)),
        ],
        out_specs=pl.BlockSpec((None, qb, A_HEADS * HEAD_DIM), lambda bb, i: (bb, i, 0)),
        out_shape=jax.ShapeDtypeStruct((b, seq, A_HEADS * HEAD_DIM), BF16),
        scratch_shapes=[
            pltpu.VMEM((seq, HEAD_DIM), BF16),
            pltpu.VMEM((seq, LANES), BF16),
            pltpu.VMEM((seq, 3 * IDX_DIM), BF16),
            pltpu.VMEM((seq // kc, qb, kc), I32),
            pltpu.VMEM((hq, HEAD_DIM), BF16),
            pltpu.VMEM((hq, LANES), F32),
            pltpu.VMEM((hq, LANES), F32),
            pltpu.VMEM((kc, kc), BF16),
            pltpu.VMEM((kc, LANES), BF16),
            pltpu.VMEM((2, hq, kc), F32),
        ],
        compiler_params=_cparams(2),
        name="dsa_topk_attention",
    )(proj3, proj3, proj3, proj3)


def _swa_kernel(sink_ref, q_ref, kp_ref, kc_ref, vp_ref, vc_ref, o_ref):
    i = pl.program_id(1)
    w = WINDOW
    group = B_HEADS // B_KV_HEADS
    q = q_ref[...]
    k = jnp.concatenate([kp_ref[...], kc_ref[...]], axis=0)
    v = jnp.concatenate([vp_ref[...], vc_ref[...]], axis=0)
    srow = lax.broadcasted_iota(I32, (group * w, 2 * w), 0)
    col = lax.broadcasted_iota(I32, (group * w, 2 * w), 1)
    kpos = (i - 1) * w + col
    diff = i * w + (srow % w) - kpos
    mask = jnp.logical_and(jnp.logical_and(diff >= 0, diff < w), kpos >= 0)
    head_of_row = lax.broadcasted_iota(I32, (group * w, 1), 0) // w
    for g in range(B_KV_HEADS):
        kg = k[:, g * HEAD_DIM:(g + 1) * HEAD_DIM].astype(BF16)
        vg = v[:, g * HEAD_DIM:(g + 1) * HEAD_DIM].astype(BF16)
        heads = [g * group + hh for hh in range(group)]
        qg = jnp.concatenate(
            [(q[:, h * HEAD_DIM:(h + 1) * HEAD_DIM] * (HEAD_DIM ** -0.5)).astype(BF16) for h in heads],
            axis=0)
        sink = jnp.zeros((group * w, 1), F32)
        for hh, h in enumerate(heads):
            sink = jnp.where(head_of_row == hh, sink_ref[h], sink)
        s = jnp.where(mask, _nt_dot(qg, kg), NEG_BIG)
        m = jnp.maximum(jnp.max(s, axis=1, keepdims=True), sink)
        p = jnp.exp(s - m)
        den = jnp.sum(p, axis=1, keepdims=True) + jnp.exp(sink - m)
        o = jnp.dot(p.astype(BF16), vg, preferred_element_type=F32) / den
        for hh, h in enumerate(heads):
            o_ref[:, h * HEAD_DIM:(h + 1) * HEAD_DIM] = o[hh * w:(hh + 1) * w].astype(o_ref.dtype)


def _swa(proj3, sinks):
    b, seq, _ = proj3.shape
    w = WINDOW
    kvw = B_KV_HEADS * HEAD_DIM
    assert kvw == LANES and seq % w == 0
    prev = lambda col: (lambda bb, i: (bb, jnp.maximum(i - 1, 0), col))
    cur = lambda col: (lambda bb, i: (bb, i, col))
    return pl.pallas_call(
        _swa_kernel,
        grid=(b, seq // w),
        in_specs=[
            pl.BlockSpec(memory_space=pltpu.SMEM),
            pl.BlockSpec((None, w, B_HEADS * HEAD_DIM), cur(COL_BQ // 512)),
            pl.BlockSpec((None, w, kvw), prev(COL_BK // LANES)),
            pl.BlockSpec((None, w, kvw), cur(COL_BK // LANES)),
            pl.BlockSpec((None, w, kvw), prev(COL_BV // LANES)),
            pl.BlockSpec((None, w, kvw), cur(COL_BV // LANES)),
        ],
        out_specs=pl.BlockSpec((None, w, B_HEADS * HEAD_DIM), lambda bb, i: (bb, i, 0)),
        out_shape=jax.ShapeDtypeStruct((b, seq, B_HEADS * HEAD_DIM), BF16),
        compiler_params=_cparams(2),
        name="swa_sink_attention",
    )(sinks, proj3, proj3, proj3, proj3, proj3)


def _moba_kernel(q_ref, k_ref, v_ref, o_ref,
                 kaug_ref, vaug_ref, kmean_ref, kmw_ref, qaug_ref, mrow_ref, acc_ref, sbuf_ref,
                 *, seq, tq, n_sel):
    i = pl.program_id(2)
    blk = MOBA_BLOCK
    nkb = seq // blk
    heads = LANES // HEAD_DIM
    assert heads == 2
    ones_lane = (HEAD_DIM, 0)

    @pl.when(i == 0)
    def _prep():
        kmean_ref[...] = jnp.zeros(kmean_ref.shape, F32)
        lane_b = lax.broadcasted_iota(I32, (blk, LANES), 1)

        def body(n, carry):
            rows = pl.ds(pl.multiple_of(n * blk, blk), blk)
            kb = k_ref[rows, :]
            vb = v_ref[rows, :]
            onehot = jnp.where(lane_b == n, 1.0, 0.0).astype(BF16)
            for e in range(heads):
                mine = (lane_b // HEAD_DIM) == e
                kaug_ref[e, rows, :] = jnp.concatenate(
                    [onehot, jnp.where(mine, kb, 0.0).astype(BF16)], axis=1)
                ones = jnp.where(lane_b == ones_lane[e], 1.0, 0.0)
                vaug_ref[e, rows, :] = jnp.where(mine, vb, ones).astype(BF16)
            kmean_ref[pl.ds(n, 1), :] = jnp.sum(kb, axis=0, keepdims=True) * (1.0 / blk)
            return carry
        lax.fori_loop(0, nkb, body, 0)
        km = kmean_ref[...]
        lane_m = lax.broadcasted_iota(I32, km.shape, 1)
        kmw = jnp.concatenate(
            [jnp.where((lane_m // HEAD_DIM) == e, km, 0.0) for e in range(heads)], axis=0)
        hi = kmw.astype(BF16)
        kmw_ref[0] = hi
        kmw_ref[1] = (kmw - hi.astype(F32)).astype(BF16)

    q = q_ref[...]
    own1 = (i * tq + lax.broadcasted_iota(I32, (tq, 1), 0)) // blk
    lane = lax.broadcasted_iota(I32, (tq, LANES), 1)
    lanef = lane.astype(F32)

    qhi = q.astype(BF16)
    qlo = (q - qhi.astype(F32)).astype(BF16)
    gates = (_nt_dot(qhi, kmw_ref[0]) + _nt_dot(qhi, kmw_ref[1])) + _nt_dot(qlo, kmw_ref[0])
    q2 = (q * Q_SCALE_LOG2).astype(BF16)
    for e in range(heads):
        g = jnp.where(lane < own1, gates[:, e * LANES:(e + 1) * LANES], -jnp.inf)
        allowed = lane == own1
        for _ in range(n_sel):
            gmax = jnp.max(g, axis=1, keepdims=True)
            is_max = jnp.logical_and(g == gmax, gmax > -jnp.inf)
            first = jnp.min(jnp.where(is_max, lanef, float(LANES)), axis=1, keepdims=True)
            onehot = lanef == first
            allowed = jnp.logical_or(allowed, onehot)
            g = jnp.where(onehot, -jnp.inf, g)
        bias = jnp.where(allowed, 0.0, NEG_BIG).astype(BF16)
        qaug_ref[e] = jnp.concatenate([bias, q2], axis=1)

    kc = tq
    ngrp = kc // LANES
    col = lax.broadcasted_iota(I32, (tq, kc), 1)
    row = lax.broadcasted_iota(I32, (tq, kc), 0)

    def chunk_rows(c):
        return pl.ds(pl.multiple_of(c * kc, kc), kc)

    def logits(c, slot):
        for e in range(heads):
            sbuf_ref[slot, e] = _nt_dot(qaug_ref[e], kaug_ref[e, chunk_rows(c), :])

    def masked(e, slot, diagonal):
        s = sbuf_ref[slot, e]
        return jnp.where(col <= row, s, NEG_BIG) if diagonal else s

    def att_chunk(c, slot, carry, is_last):
        for e in range(heads):
            s = masked(e, slot, is_last)
            m_old = mrow_ref[e]
            mx = s[:, :LANES]
            for gi in range(1, ngrp):
                mx = jnp.maximum(mx, s[:, gi * LANES:(gi + 1) * LANES])
            m_new = jnp.maximum(m_old, jnp.broadcast_to(jnp.max(mx, axis=1, keepdims=True), (tq, LANES)))
            p = jnp.concatenate(
                [jnp.exp2(s[:, gi * LANES:(gi + 1) * LANES] - m_new).astype(BF16) for gi in range(ngrp)],
                axis=1)
            acc_ref[e] = jnp.exp2(m_old - m_new) * acc_ref[e] + jnp.dot(
                p, vaug_ref[e, chunk_rows(c), :], preferred_element_type=F32)
            mrow_ref[e] = m_new
        return carry
    mrow_ref[...] = jnp.full(mrow_ref.shape, NEG_BIG, F32)
    acc_ref[...] = jnp.zeros(acc_ref.shape, F32)
    _pipelined(i + 1, logits, att_chunk, 0)

    outs = []
    for e in range(heads):
        acc = acc_ref[e]
        outs.append(acc / acc[:, ones_lane[e]:ones_lane[e] + 1])
    o_ref[...] = jnp.where(lane < HEAD_DIM, outs[0], outs[1]).astype(o_ref.dtype)


def _moba(proj3, tq=512):
    b, seq, _ = proj3.shape
    tq = min(tq, seq)
    blk = MOBA_BLOCK
    assert seq % blk == 0 and tq % blk == 0 and seq % tq == 0
    nkb = seq // blk
    assert nkb <= LANES
    n_sel = min(MOBA_TOPK, nkb - 1)
    heads = LANES // HEAD_DIM
    kern = functools.partial(_moba_kernel, seq=seq, tq=tq, n_sel=n_sel)
    return pl.pallas_call(
        kern,
        grid=(b, C_HEADS // heads, seq // tq),
        in_specs=[
            pl.BlockSpec((None, tq, LANES), lambda bb, hp, i: (bb, i, COL_CQ // LANES + hp)),
            pl.BlockSpec((None, seq, LANES), lambda bb, hp, i: (bb, 0, COL_CK // LANES + hp)),
            pl.BlockSpec((None, seq, LANES), lambda bb, hp, i: (bb, 0, COL_CV // LANES + hp)),
        ],
        out_specs=pl.BlockSpec((None, tq, LANES), lambda bb, hp, i: (bb, i, hp)),
        out_shape=jax.ShapeDtypeStruct((b, seq, C_HEADS * HEAD_DIM), BF16),
        scratch_shapes=[
            pltpu.VMEM((heads, seq, 2 * LANES), BF16),
            pltpu.VMEM((heads, seq, LANES), BF16),
            pltpu.VMEM((LANES, LANES), F32),
            pltpu.VMEM((2, heads * LANES, LANES), BF16),
            pltpu.VMEM((heads, tq, 2 * LANES), BF16),
            pltpu.VMEM((heads, tq, LANES), F32),
            pltpu.VMEM((heads, tq, LANES), F32),
            pltpu.VMEM((2, heads, tq, tq), F32),
        ],
        compiler_params=_cparams(3),
        name="moba_attention",
    )(proj3, proj3, proj3)


def _merge_kernel(x_ref, oa_ref, ob_ref, oc_ref, g_ref, wb_ref, wo_ref, o_ref):
    merged = None
    for n, br in enumerate((oa_ref, ob_ref, oc_ref)):
        y = jnp.dot(br[...], wb_ref[n], preferred_element_type=F32)
        term = jax.nn.sigmoid(g_ref[:, n * D_MODEL:(n + 1) * D_MODEL]) * y
        merged = term if merged is None else merged + term
    o_ref[...] = x_ref[...] + jnp.dot(merged.astype(BF16), wo_ref[...], preferred_element_type=F32)


def _merge(x2, oa, ob, oc, proj2, w_branch, w_out, tm=512):
    t = x2.shape[0]
    assert t % tm == 0
    row = lambda i: (i, 0)
    return pl.pallas_call(
        _merge_kernel,
        grid=(t // tm,),
        in_specs=[
            pl.BlockSpec((tm, D_MODEL), row),
            pl.BlockSpec((tm, BRANCH_WIDTH), row),
            pl.BlockSpec((tm, BRANCH_WIDTH), row),
            pl.BlockSpec((tm, BRANCH_WIDTH), row),
            pl.BlockSpec((tm, N_BRANCH * D_MODEL), lambda i: (i, COL_GATES // (N_BRANCH * D_MODEL))),
            pl.BlockSpec((N_BRANCH, BRANCH_WIDTH, D_MODEL), lambda i: (0, 0, 0)),
            pl.BlockSpec((D_MODEL, D_MODEL), lambda i: (0, 0)),
        ],
        out_specs=pl.BlockSpec((tm, D_MODEL), row),
        out_shape=jax.ShapeDtypeStruct((t, D_MODEL), F32),
        compiler_params=_cparams(1),
        name="gated_merge_out_proj",
    )(x2, oa, ob, oc, proj2, w_branch, w_out)


def _ffn_kernel(x_ref, g_ref, w1_ref, w2_ref, gf_ref, o_ref, h_ref, acc_ref, *, final_norm):
    j = pl.program_id(1)

    @pl.when(j == 0)
    def _():
        x = x_ref[...]
        ms = jnp.mean(x * x, axis=-1, keepdims=True)
        h_ref[...] = (x * lax.rsqrt(ms + NORM_EPS) * g_ref[...]).astype(BF16)
        acc_ref[...] = jnp.zeros(acc_ref.shape, F32)

    u = jnp.maximum(jnp.dot(h_ref[...], w1_ref[...], preferred_element_type=F32), 0.0)
    acc_ref[...] += jnp.dot((u * u).astype(BF16), w2_ref[...], preferred_element_type=F32)

    @pl.when(j == pl.num_programs(1) - 1)
    def _():
        y = x_ref[...] + acc_ref[...]
        if final_norm:
            ms = jnp.mean(y * y, axis=-1, keepdims=True)
            y = y * lax.rsqrt(ms + NORM_EPS) * gf_ref[...]
        o_ref[...] = y


def _ffn(x2, gain, w1, w2, gain_f, final_norm, tm=512, tf=1024):
    t = x2.shape[0]
    assert t % tm == 0 and D_FF % tf == 0
    kern = functools.partial(_ffn_kernel, final_norm=final_norm)
    return pl.pallas_call(
        kern,
        grid=(t // tm, D_FF // tf),
        in_specs=[
            pl.BlockSpec((tm, D_MODEL), lambda i, j: (i, 0)),
            pl.BlockSpec((1, D_MODEL), lambda i, j: (0, 0)),
            pl.BlockSpec((D_MODEL, tf), lambda i, j: (0, j)),
            pl.BlockSpec((tf, D_MODEL), lambda i, j: (j, 0)),
            pl.BlockSpec((1, D_MODEL), lambda i, j: (0, 0)),
        ],
        out_specs=pl.BlockSpec((tm, D_MODEL), lambda i, j: (i, 0)),
        out_shape=jax.ShapeDtypeStruct((t, D_MODEL), F32),
        scratch_shapes=[pltpu.VMEM((tm, D_MODEL), BF16), pltpu.VMEM((tm, D_MODEL), F32)],
        compiler_params=_cparams(2),
        name="relu2_mlp",
    )(x2, gain, w1, w2, gain_f)


def _pack_w_in(w_in):
    sizes = (A_HEADS * HEAD_DIM, HEAD_DIM, HEAD_DIM, IDX_HEADS * IDX_DIM, IDX_DIM, IDX_HEADS,
             B_HEADS * HEAD_DIM, B_KV_HEADS * HEAD_DIM, B_KV_HEADS * HEAD_DIM,
             C_HEADS * HEAD_DIM, C_HEADS * HEAD_DIM, C_HEADS * HEAD_DIM, N_BRANCH * D_MODEL)
    offs = np.concatenate([[0], np.cumsum(sizes)])
    w_in = w_in.astype(BF16)
    aq, ak, av, iq, ik, iw, bq, bk, bv, cq, ck, cv, gates = [
        w_in[..., offs[n]:offs[n + 1]] for n in range(len(sizes))]
    lead = w_in.shape[:-1]
    pad_iw = jnp.zeros(lead + (LANES - HEAD_DIM - IDX_HEADS,), BF16)
    parts = [gates, aq, bq, cq, ck, iq, bk, ak, ik, av, iw, pad_iw, bv, cv]
    used = sum(p.shape[-1] for p in parts)
    parts.append(jnp.zeros(lead + (N_PACKED - used,), BF16))
    return jnp.concatenate(parts, axis=-1)


def _rope_tables(seq):
    half = HEAD_DIM // 2
    inv = 1.0 / (ROPE_THETA ** (jnp.arange(0, HEAD_DIM, 2, dtype=F32) / HEAD_DIM))
    ang = jnp.arange(seq, dtype=F32)[:, None] * inv[None, :]
    cos, sin = jnp.cos(ang), jnp.sin(ang)
    reps = LANES // HEAD_DIM
    cos_t = jnp.tile(jnp.concatenate([cos, cos], axis=1), (1, reps))
    sin_t = jnp.tile(jnp.concatenate([-sin, sin], axis=1), (1, reps))
    assert cos_t.shape == (seq, LANES) and half * 2 == HEAD_DIM
    return cos_t, sin_t


def kernel(x, norm1, w_in, sinks, w_branch, w_out, norm2, w_ff1, w_ff2, norm_f):
    b, seq, d = x.shape
    depth = w_in.shape[0]
    cos_t, sin_t = _rope_tables(seq)
    w_in_p = _pack_w_in(w_in)
    w_branch_b = w_branch.astype(BF16)
    w_out_b = w_out.astype(BF16)
    w_ff1_b = w_ff1.astype(BF16)
    w_ff2_b = w_ff2.astype(BF16)
    gain_f = norm_f.reshape(1, d)

    x2 = x.reshape(b * seq, d)
    for l in range(depth):
        proj2 = _project(x2, norm1[l].reshape(1, d), w_in_p[l], cos_t, sin_t, seq)
        proj3 = proj2.reshape(b, seq, N_PACKED)
        o_a = _dsa(proj3).reshape(b * seq, BRANCH_WIDTH)
        o_b = _swa(proj3, sinks[l]).reshape(b * seq, BRANCH_WIDTH)
        o_c = _moba(proj3).reshape(b * seq, BRANCH_WIDTH)
        x2 = _merge(x2, o_a, o_b, o_c, proj2, w_branch_b[l], w_out_b[l])
        x2 = _ffn(x2, norm2[l].reshape(1, d), w_ff1_b[l], w_ff2_b[l], gain_f,
                  final_norm=(l == depth - 1))
    return x2.reshape(b, seq, d)
```

```python
import functools

import jax
import jax.numpy as jnp
import numpy as np
from jax import lax
from jax.experimental import pallas as pl
from jax.experimental.pallas import tpu as pltpu

F32 = jnp.float32
BF16 = jnp.bfloat16
I32 = jnp.int32

D_MODEL = 1024
HEAD_DIM = 64
ROPE_THETA = 10000.0
NORM_EPS = 1e-6
A_HEADS = 8
IDX_HEADS = 4
IDX_DIM = 64
IDX_TOPK_MAX = 256
B_HEADS = 8
B_KV_HEADS = 2
WINDOW = 128
C_HEADS = 8
MOBA_BLOCK = 256
MOBA_TOPK = 3
Q_BLOCK = 128
N_BRANCH = 3
BRANCH_WIDTH = 8 * HEAD_DIM
D_FF = 4 * D_MODEL

LANES = 128
NEG_BIG = -1e30
Q_SCALE_LOG2 = (HEAD_DIM ** -0.5) * float(np.log2(np.e))
INT_MIN = -(2 ** 31)
BISECT_UNCHECKED_STEPS = 14

COL_GATES = 0
COL_AQ = 3072
COL_BQ = 3584
COL_CQ = 4096
COL_CK = 4608
COL_IQ = 5120
COL_BK = 5376
COL_AKIK = 5504
ROPE_BEGIN, ROPE_END = 3072, 5632
COL_AVIW = 5632
COL_BV = 5760
COL_CV = 5888
N_PACKED = 6656

VMEM_LIMIT = 56 * 1024 * 1024


def _cparams(n_axes):
    return pltpu.CompilerParams(dimension_semantics=("arbitrary",) * n_axes,
                                vmem_limit_bytes=VMEM_LIMIT)


def _nt_dot(a, b):
    return lax.dot_general(a, b, (((1,), (1,)), ((), ())), preferred_element_type=F32)


def _pipelined(n, produce, consume, init):
    produce(0, 0)

    def pair(t, carry):
        c = 2 * t
        produce(c + 1, 1)
        carry = consume(c, 0, carry, False)
        produce(c + 2, 0)
        return consume(c + 1, 1, carry, False)
    n_pairs = (n - 1) // 2
    carry = lax.fori_loop(0, n_pairs, pair, init)
    c0 = 2 * n_pairs

    def two_left(carry):
        produce(c0 + 1, 1)
        carry = consume(c0, 0, carry, False)
        return consume(c0 + 1, 1, carry, True)

    def one_left(carry):
        return consume(c0, 0, carry, True)
    return lax.cond(n - c0 == 2, two_left, one_left, carry)


def _split3_lhs(x):
    hi = x.astype(BF16)
    lo = (x - hi.astype(F32)).astype(BF16)
    return jnp.concatenate([hi, hi, lo], axis=1)


def _split3_rhs(x):
    hi = x.astype(BF16)
    lo = (x - hi.astype(F32)).astype(BF16)
    return jnp.concatenate([hi, lo, hi], axis=1)


def _proj_kernel(x_ref, g_ref, w_ref, cos_ref, sin_ref, o_ref, h_ref, *, rope_lo, rope_hi, tn):
    j = pl.program_id(1)

    @pl.when(j == 0)
    def _():
        x = x_ref[...]
        ms = jnp.mean(x * x, axis=-1, keepdims=True)
        h_ref[...] = (x * lax.rsqrt(ms + NORM_EPS) * g_ref[...]).astype(BF16)

    acc = jnp.dot(h_ref[...], w_ref[...], preferred_element_type=F32)
    is_rope = jnp.logical_and(j >= rope_lo, j < rope_hi)

    @pl.when(is_rope)
    def _():
        cos = cos_ref[...]
        sin = sin_ref[...]
        lane = lax.broadcasted_iota(I32, cos.shape, 1)
        first_half = (lane % HEAD_DIM) < (HEAD_DIM // 2)
        for c in range(tn // LANES):
            a = acc[:, c * LANES:(c + 1) * LANES]
            partner = jnp.where(first_half,
                                pltpu.roll(a, LANES - HEAD_DIM // 2, 1),
                                pltpu.roll(a, HEAD_DIM // 2, 1))
            o_ref[:, c * LANES:(c + 1) * LANES] = a * cos + partner * sin

    @pl.when(jnp.logical_not(is_rope))
    def _():
        o_ref[...] = acc


def _project(x2, gain, w_packed, cos_t, sin_t, seq, tm=1024, tn=512):
    t = x2.shape[0]
    tm = min(tm, seq)
    n = w_packed.shape[1]
    assert t % tm == 0 and seq % tm == 0 and n % tn == 0
    assert ROPE_BEGIN % tn == 0 and ROPE_END % tn == 0
    pos_blocks = seq // tm
    kern = functools.partial(_proj_kernel, rope_lo=ROPE_BEGIN // tn, rope_hi=ROPE_END // tn, tn=tn)
    return pl.pallas_call(
        kern,
        grid=(t // tm, n // tn),
        in_specs=[
            pl.BlockSpec((tm, D_MODEL), lambda i, j: (i, 0)),
            pl.BlockSpec((1, D_MODEL), lambda i, j: (0, 0)),
            pl.BlockSpec((D_MODEL, tn), lambda i, j: (0, j)),
            pl.BlockSpec((tm, LANES), lambda i, j: (i % pos_blocks, 0)),
            pl.BlockSpec((tm, LANES), lambda i, j: (i % pos_blocks, 0)),
        ],
        out_specs=pl.BlockSpec((tm, tn), lambda i, j: (i, j)),
        out_shape=jax.ShapeDtypeStruct((t, n), F32),
        scratch_shapes=[pltpu.VMEM((tm, D_MODEL), BF16)],
        compiler_params=_cparams(2),
        name="norm_proj_rope",
    )(x2, gain, w_packed, cos_t, sin_t)


def _value_with_ones(v):
    lane = lax.broadcasted_iota(I32, v.shape, 1)
    return jnp.concatenate([v.astype(BF16), jnp.where(lane == 0, 1.0, 0.0).astype(BF16)], axis=1)


def _dsa_kernel(aq_ref, iq_ref, kk_ref, vw_ref, o_ref,
                kbf_ref, vbf_ref, ikc_ref, keys_ref, qall_ref, mrow_ref, acc_ref, tri_ref, ones_ref,
                sbuf_ref, *, seq, kc, qb, top_k):
    i = pl.program_id(1)
    n_prep = seq // kc
    idx_scale = (IDX_DIM ** -0.5) * (IDX_HEADS ** -0.5)

    @pl.when(i == 0)
    def _prep():
        def body(c, carry):
            rows = pl.ds(pl.multiple_of(c * kc, kc), kc)
            kk = kk_ref[rows, :]
            kbf_ref[rows, :] = kk[:, :HEAD_DIM].astype(BF16)
            ikc_ref[rows, :] = _split3_rhs(kk[:, HEAD_DIM:])
            vbf_ref[rows, :] = _value_with_ones(vw_ref[rows, :HEAD_DIM])
            return carry
        lax.fori_loop(0, n_prep, body, 0)
        r = lax.broadcasted_iota(I32, (kc, kc), 0)
        cidx = lax.broadcasted_iota(I32, (kc, kc), 1)
        tri_ref[...] = jnp.where(r > cidx, 1.0, 0.0).astype(BF16)
        ones_ref[...] = jnp.ones(ones_ref.shape, BF16)

    nkc = (i * qb) // kc + 1

    iq = iq_ref[...]
    w = vw_ref[pl.ds(pl.multiple_of(i * qb, qb), qb), HEAD_DIM:HEAD_DIM + IDX_HEADS] * idx_scale
    qcat = jnp.concatenate(
        [_split3_lhs(iq[:, h * IDX_DIM:(h + 1) * IDX_DIM]) for h in range(IDX_HEADS)], axis=0)
    row = lax.broadcasted_iota(I32, (qb, LANES), 0)
    lane = lax.broadcasted_iota(I32, (qb, LANES), 1)
    qpos = i * qb + row
    w_b = [jnp.broadcast_to(w[:, h:h + 1], (qb, LANES)) for h in range(IDX_HEADS)]
    ngrp = kc // LANES

    def chunk_rows(c):
        return pl.ds(pl.multiple_of(c * kc, kc), kc)

    def to_key(x):
        bits = lax.bitcast_convert_type(x, I32)
        return jnp.where(bits < 0, bits ^ jnp.int32(0x7FFFFFFF), bits)

    def idx_dots(c, slot):
        sbuf_ref[slot, :IDX_HEADS * qb] = _nt_dot(qcat, ikc_ref[chunk_rows(c), :])

    def score_chunk(c, slot, smax, is_last):
        for g in range(ngrp):
            sc = None
            for h in range(IDX_HEADS):
                d = sbuf_ref[slot, h * qb:(h + 1) * qb, g * LANES:(g + 1) * LANES]
                term = jnp.maximum(d, 0.0) * w_b[h]
                sc = term if sc is None else sc + term
            sc = jnp.where(sc == 0.0, 0.0, sc)
            key = to_key(sc)
            if is_last:
                causal = c * kc + g * LANES + lane <= qpos
                key = jnp.where(causal, key, jnp.int32(INT_MIN))
                sc = jnp.where(causal, sc, -jnp.inf)
            keys_ref[c, :, g * LANES:(g + 1) * LANES] = key
            smax = jnp.maximum(smax, sc)
        return smax

    smax = _pipelined(nkc, idx_dots, score_chunk, jnp.full((qb, LANES), -jnp.inf, F32))
    key_max = to_key(jnp.max(smax, axis=1, keepdims=True))

    def count(pred, thr):
        parts = []
        for r0 in range(0, qb, LANES):
            thr_b = jnp.broadcast_to(thr[r0:r0 + LANES], (LANES, LANES))

            def body(c, acc, r0=r0, thr_b=thr_b):
                kch = keys_ref[c, r0:r0 + LANES, :]
                for g in range(kc // LANES):
                    acc = acc + jnp.where(pred(kch[:, g * LANES:(g + 1) * LANES], thr_b), 1.0, 0.0)
                return acc
            parts.append(lax.fori_loop(0, nkc, body, jnp.zeros((LANES, LANES), F32)))
        return jnp.sum(jnp.concatenate(parts, axis=0), axis=1, keepdims=True)

    ge = lambda k, t: k >= t
    kf = float(top_k)

    def probe(v, state):
        lo, n_lo, hi = state
        n_v = count(ge, v)
        inside = jnp.logical_and(v > lo, v < hi)
        up = jnp.logical_and(inside, n_v >= kf)
        down = jnp.logical_and(inside, n_v < kf)
        return jnp.where(up, v, lo), jnp.where(up, n_v, n_lo), jnp.where(down, v, hi)

    def bisect(state):
        lo, _, hi = state
        return probe(lo + lax.shift_right_logical(hi - lo, jnp.int32(1)), state)

    def unfinished(state):
        lo, n_lo, hi = state
        done = jnp.logical_or(n_lo == kf, hi - 1 == lo)
        return jnp.max(jnp.where(done, 0.0, 1.0)) > 0.0

    state = (jnp.full((qb, 1), INT_MIN, I32), jnp.full((qb, 1), float(seq + 1), F32), key_max + 1)
    state = probe(jnp.ones((qb, 1), I32), state)
    state = probe(jnp.zeros((qb, 1), I32), state)
    two_binades = jnp.int32(1 << 24)
    state = probe(jnp.maximum(key_max, jnp.int32(INT_MIN) + two_binades) - two_binades, state)
    state = lax.fori_loop(0, BISECT_UNCHECKED_STEPS, lambda _, st: bisect(st), state)

    def bisect_cond(carry):
        return jnp.logical_and(carry[0], carry[1] < 34)

    def bisect_body(carry):
        state = bisect(carry[2:])
        return (unfinished(state), carry[1] + 1) + state
    carry = lax.while_loop(bisect_cond, bisect_body, (unfinished(state), jnp.int32(0)) + state)
    thr, n_thr = carry[2], carry[3]

    excess = jnp.where(thr > jnp.int32(INT_MIN), n_thr - kf, float(seq + 1))
    thr_b = jnp.broadcast_to(thr, (qb, LANES))
    excess_b = jnp.broadcast_to(excess, (qb, LANES))

    aq = aq_ref[...]
    for h in range(A_HEADS):
        qall_ref[h * qb:(h + 1) * qb, :] = (
            aq[:, h * HEAD_DIM:(h + 1) * HEAD_DIM] * Q_SCALE_LOG2).astype(BF16)

    def logits(c, slot):
        sbuf_ref[slot] = _nt_dot(qall_ref[...], kbf_ref[chunk_rows(c), :])

    def later_ties(step, slot):
        kch = keys_ref[nkc - 1 - step]
        eq_b = jnp.concatenate(
            [jnp.where(kch[:, g * LANES:(g + 1) * LANES] == thr_b, 1.0, 0.0).astype(BF16)
             for g in range(ngrp)], axis=1)
        sbuf_ref[slot, :qb] = jnp.dot(eq_b, tri_ref[...], preferred_element_type=F32)
        sbuf_ref[slot, qb:2 * qb, :LANES] = jnp.dot(eq_b, ones_ref[...], preferred_element_type=F32)

    def bias_chunk(step, slot, ties_after, is_last):
        c = nkc - 1 - step
        kch = keys_ref[c]
        groups = []
        for g in range(ngrp):
            k_g = kch[:, g * LANES:(g + 1) * LANES]
            later = sbuf_ref[slot, :qb, g * LANES:(g + 1) * LANES] + ties_after
            keep = jnp.logical_or(k_g > thr_b, jnp.logical_and(k_g == thr_b, later >= excess_b))
            groups.append(jnp.where(keep, 0.0, NEG_BIG))
        keys_ref[c] = lax.bitcast_convert_type(jnp.concatenate(groups, axis=1), I32)
        return ties_after + sbuf_ref[slot, qb:2 * qb, :LANES]
    _pipelined(nkc, later_ties, bias_chunk, jnp.zeros((qb, LANES), F32))

    def masked(slot, bias):
        return (sbuf_ref[slot].reshape(A_HEADS, qb, kc) + bias[None]).reshape(A_HEADS * qb, kc)

    def att_chunk(c, slot, carry, is_last):
        s = masked(slot, lax.bitcast_convert_type(keys_ref[c], F32))
        m_old = mrow_ref[...]
        mx = s[:, :LANES]
        for g in range(1, ngrp):
            mx = jnp.maximum(mx, s[:, g * LANES:(g + 1) * LANES])
        m_new = jnp.maximum(m_old, jnp.broadcast_to(jnp.max(mx, axis=1, keepdims=True), m_old.shape))
        p = jnp.concatenate(
            [jnp.exp2(s[:, g * LANES:(g + 1) * LANES] - m_new).astype(BF16) for g in range(ngrp)], axis=1)
        acc_ref[...] = jnp.exp2(m_old - m_new) * acc_ref[...] + jnp.dot(
            p, vbf_ref[chunk_rows(c), :], preferred_element_type=F32)
        mrow_ref[...] = m_new
        return carry
    mrow_ref[...] = jnp.full(mrow_ref.shape, NEG_BIG, F32)
    acc_ref[...] = jnp.zeros(acc_ref.shape, F32)
    _pipelined(nkc, logits, att_chunk, 0)

    acc = acc_ref[...]
    out = acc[:, :HEAD_DIM] / acc[:, HEAD_DIM:HEAD_DIM + 1]
    for h in range(A_HEADS):
        o_ref[:, h * HEAD_DIM:(h + 1) * HEAD_DIM] = out[h * qb:(h + 1) * qb].astype(o_ref.dtype)


def _dsa(proj3, kc=512, qb=256):
    b, seq, _ = proj3.shape
    top_k = min(IDX_TOPK_MAX, seq // 4)
    kc = min(kc, seq)
    assert seq % kc == 0 and kc % qb == 0 and seq % qb == 0 and kc >= top_k
    kern = functools.partial(_dsa_kernel, seq=seq, kc=kc, qb=qb, top_k=top_k)
    hq = A_HEADS * qb
    return pl.pallas_call(
        kern,
        grid=(b, seq // qb),
        in_specs=[
            pl.BlockSpec((None, qb, A_HEADS * HEAD_DIM), lambda bb, i: (bb, i, COL_AQ // 512)),
            pl.BlockSpec((None, qb, IDX_HEADS * IDX_DIM), lambda bb, i: (bb, i, COL_IQ // 256)),
            pl.BlockSpec((None, seq, LANES), lambda bb, i: (bb, 0, COL_AKIK // LANES),
                         pipeline_mode=pl.Buffered(1)),
            pl.BlockSpec((None, seq, LANES), lambda bb, i: (bb, 0, COL_AVIW // LANES),
                         pipeline_mode=pl.Buffered(1)),
        ],
        out_specs=pl.BlockSpec((None, qb, A_HEADS * HEAD_DIM), lambda bb, i: (bb, i, 0)),
        out_shape=jax.ShapeDtypeStruct((b, seq, A_HEADS * HEAD_DIM), BF16),
        scratch_shapes=[
            pltpu.VMEM((seq, HEAD_DIM), BF16),
            pltpu.VMEM((seq, LANES), BF16),
            pltpu.VMEM((seq, 3 * IDX_DIM), BF16),
            pltpu.VMEM((seq // kc, qb, kc), I32),
            pltpu.VMEM((hq, HEAD_DIM), BF16),
            pltpu.VMEM((hq, LANES), F32),
            pltpu.VMEM((hq, LANES), F32),
            pltpu.VMEM((kc, kc), BF16),
            pltpu.VMEM((kc, LANES), BF16),
            pltpu.VMEM((2, hq, kc), F32),
        ],
        compiler_params=_cparams(2),
        name="dsa_topk_attention",
    )(proj3, proj3, proj3, proj3)


def _swa_kernel(sink_ref, q_ref, kp_ref, kc_ref, vp_ref, vc_ref, o_ref):
    i = pl.program_id(1)
    w = WINDOW
    group = B_HEADS // B_KV_HEADS
    q = q_ref[...]
    k = jnp.concatenate([kp_ref[...], kc_ref[...]], axis=0)
    v = jnp.concatenate([vp_ref[...], vc_ref[...]], axis=0)
    srow = lax.broadcasted_iota(I32, (group * w, 2 * w), 0)
    col = lax.broadcasted_iota(I32, (group * w, 2 * w), 1)
    kpos = (i - 1) * w + col
    diff = i * w + (srow % w) - kpos
    mask = jnp.logical_and(jnp.logical_and(diff >= 0, diff < w), kpos >= 0)
    head_of_row = lax.broadcasted_iota(I32, (group * w, 1), 0) // w
    for g in range(B_KV_HEADS):
        kg = k[:, g * HEAD_DIM:(g + 1) * HEAD_DIM].astype(BF16)
        vg = v[:, g * HEAD_DIM:(g + 1) * HEAD_DIM].astype(BF16)
        heads = [g * group + hh for hh in range(group)]
        qg = jnp.concatenate(
            [(q[:, h * HEAD_DIM:(h + 1) * HEAD_DIM] * (HEAD_DIM ** -0.5)).astype(BF16) for h in heads],
            axis=0)
        sink = jnp.zeros((group * w, 1), F32)
        for hh, h in enumerate(heads):
            sink = jnp.where(head_of_row == hh, sink_ref[h], sink)
        s = jnp.where(mask, _nt_dot(qg, kg), NEG_BIG)
        m = jnp.maximum(jnp.max(s, axis=1, keepdims=True), sink)
        p = jnp.exp(s - m)
        den = jnp.sum(p, axis=1, keepdims=True) + jnp.exp(sink - m)
        o = jnp.dot(p.astype(BF16), vg, preferred_element_type=F32) / den
        for hh, h in enumerate(heads):
            o_ref[:, h * HEAD_DIM:(h + 1) * HEAD_DIM] = o[hh * w:(hh + 1) * w].astype(o_ref.dtype)


def _swa(proj3, sinks):
    b, seq, _ = proj3.shape
    w = WINDOW
    kvw = B_KV_HEADS * HEAD_DIM
    assert kvw == LANES and seq % w == 0
    prev = lambda col: (lambda bb, i: (bb, jnp.maximum(i - 1, 0), col))
    cur = lambda col: (lambda bb, i: (bb, i, col))
    return pl.pallas_call(
        _swa_kernel,
        grid=(b, seq // w),
        in_specs=[
            pl.BlockSpec(memory_space=pltpu.SMEM),
            pl.BlockSpec((None, w, B_HEADS * HEAD_DIM), cur(COL_BQ // 512)),
            pl.BlockSpec((None, w, kvw), prev(COL_BK // LANES)),
            pl.BlockSpec((None, w, kvw), cur(COL_BK // LANES)),
            pl.BlockSpec((None, w, kvw), prev(COL_BV // LANES)),
            pl.BlockSpec((None, w, kvw), cur(COL_BV // LANES)),
        ],
        out_specs=pl.BlockSpec((None, w, B_HEADS * HEAD_DIM), lambda bb, i: (bb, i, 0)),
        out_shape=jax.ShapeDtypeStruct((b, seq, B_HEADS * HEAD_DIM), BF16),
        compiler_params=_cparams(2),
        name="swa_sink_attention",
    )(sinks, proj3, proj3, proj3, proj3, proj3)


def _moba_kernel(q_ref, k_ref, v_ref, o_ref,
                 kaug_ref, vaug_ref, kmean_ref, kmw_ref, qaug_ref, mrow_ref, acc_ref, sbuf_ref,
                 *, seq, tq, n_sel):
    i = pl.program_id(2)
    blk = MOBA_BLOCK
    nkb = seq // blk
    heads = LANES // HEAD_DIM
    assert heads == 2
    ones_lane = (HEAD_DIM, 0)

    @pl.when(i == 0)
    def _prep():
        kmean_ref[...] = jnp.zeros(kmean_ref.shape, F32)
        lane_b = lax.broadcasted_iota(I32, (blk, LANES), 1)

        def body(n, carry):
            rows = pl.ds(pl.multiple_of(n * blk, blk), blk)
            kb = k_ref[rows, :]
            vb = v_ref[rows, :]
            onehot = jnp.where(lane_b == n, 1.0, 0.0).astype(BF16)
            for e in range(heads):
                mine = (lane_b // HEAD_DIM) == e
                kaug_ref[e, rows, :] = jnp.concatenate(
                    [onehot, jnp.where(mine, kb, 0.0).astype(BF16)], axis=1)
                ones = jnp.where(lane_b == ones_lane[e], 1.0, 0.0)
                vaug_ref[e, rows, :] = jnp.where(mine, vb, ones).astype(BF16)
            kmean_ref[pl.ds(n, 1), :] = jnp.sum(kb, axis=0, keepdims=True) * (1.0 / blk)
            return carry
        lax.fori_loop(0, nkb, body, 0)
        km = kmean_ref[...]
        lane_m = lax.broadcasted_iota(I32, km.shape, 1)
        kmw = jnp.concatenate(
            [jnp.where((lane_m // HEAD_DIM) == e, km, 0.0) for e in range(heads)], axis=0)
        hi = kmw.astype(BF16)
        kmw_ref[0] = hi
        kmw_ref[1] = (kmw - hi.astype(F32)).astype(BF16)

    q = q_ref[...]
    lane = lax.broadcasted_iota(I32, (tq, LANES), 1)

    qhi = q.astype(BF16)
    qlo = (q - qhi.astype(F32)).astype(BF16)
    gates_t = (_nt_dot(kmw_ref[0], qhi) + _nt_dot(kmw_ref[1], qhi)) + _nt_dot(kmw_ref[0], qlo)
    q2 = (q * Q_SCALE_LOG2).astype(BF16)
    nb8 = -(-nkb // 8) * 8
    own_t = (i * tq + lax.broadcasted_iota(I32, (nb8, tq), 1)) // blk
    blk_t = lax.broadcasted_iota(I32, (nb8, tq), 0)
    blk_tf = blk_t.astype(F32)
    for e in range(heads):
        g = jnp.where(blk_t < own_t, gates_t[e * LANES:e * LANES + nb8], -jnp.inf)
        allowed = blk_t == own_t
        for _ in range(n_sel):
            gmax = jnp.max(g, axis=0, keepdims=True)
            is_max = jnp.logical_and(g == gmax, gmax > -jnp.inf)
            first = jnp.min(jnp.where(is_max, blk_tf, float(LANES)), axis=0, keepdims=True)
            onehot = blk_tf == first
            allowed = jnp.logical_or(allowed, onehot)
            g = jnp.where(onehot, -jnp.inf, g)
        bias_t = jnp.where(allowed, 0.0, NEG_BIG)
        if nb8 < LANES:
            bias_t = jnp.concatenate([bias_t, jnp.full((LANES - nb8, tq), NEG_BIG, F32)], axis=0)
        qaug_ref[e] = jnp.concatenate([bias_t.T.astype(BF16), q2], axis=1)

    kc = tq
    ngrp = kc // LANES
    col = lax.broadcasted_iota(I32, (tq, kc), 1)
    row = lax.broadcasted_iota(I32, (tq, kc), 0)

    def chunk_rows(c):
        return pl.ds(pl.multiple_of(c * kc, kc), kc)

    def logits(c, slot):
        for e in range(heads):
            sbuf_ref[slot, e] = _nt_dot(qaug_ref[e], kaug_ref[e, chunk_rows(c), :])

    def masked(e, slot, diagonal):
        s = sbuf_ref[slot, e]
        return jnp.where(col <= row, s, NEG_BIG) if diagonal else s

    def att_chunk(c, slot, carry, is_last):
        for e in range(heads):
            s = masked(e, slot, is_last)
            m_old = mrow_ref[e]
            mx = s[:, :LANES]
            for gi in range(1, ngrp):
                mx = jnp.maximum(mx, s[:, gi * LANES:(gi + 1) * LANES])
            m_new = jnp.maximum(m_old, jnp.broadcast_to(jnp.max(mx, axis=1, keepdims=True), (tq, LANES)))
            p = jnp.concatenate(
                [jnp.exp2(s[:, gi * LANES:(gi + 1) * LANES] - m_new).astype(BF16) for gi in range(ngrp)],
                axis=1)
            acc_ref[e] = jnp.exp2(m_old - m_new) * acc_ref[e] + jnp.dot(
                p, vaug_ref[e, chunk_rows(c), :], preferred_element_type=F32)
            mrow_ref[e] = m_new
        return carry
    mrow_ref[...] = jnp.full(mrow_ref.shape, NEG_BIG, F32)
    acc_ref[...] = jnp.zeros(acc_ref.shape, F32)
    _pipelined(i + 1, logits, att_chunk, 0)

    outs = []
    for e in range(heads):
        acc = acc_ref[e]
        outs.append(acc / acc[:, ones_lane[e]:ones_lane[e] + 1])
    o_ref[...] = jnp.where(lane < HEAD_DIM, outs[0], outs[1]).astype(o_ref.dtype)


def _moba(proj3, tq=512):
    b, seq, _ = proj3.shape
    tq = min(tq, seq)
    blk = MOBA_BLOCK
    assert seq % blk == 0 and tq % blk == 0 and seq % tq == 0
    nkb = seq // blk
    assert nkb <= LANES
    n_sel = min(MOBA_TOPK, nkb - 1)
    heads = LANES // HEAD_DIM
    kern = functools.partial(_moba_kernel, seq=seq, tq=tq, n_sel=n_sel)
    return pl.pallas_call(
        kern,
        grid=(b, C_HEADS // heads, seq // tq),
        in_specs=[
            pl.BlockSpec((None, tq, LANES), lambda bb, hp, i: (bb, i, COL_CQ // LANES + hp)),
            pl.BlockSpec((None, seq, LANES), lambda bb, hp, i: (bb, 0, COL_CK // LANES + hp)),
            pl.BlockSpec((None, seq, LANES), lambda bb, hp, i: (bb, 0, COL_CV // LANES + hp)),
        ],
        out_specs=pl.BlockSpec((None, tq, LANES), lambda bb, hp, i: (bb, i, hp)),
        out_shape=jax.ShapeDtypeStruct((b, seq, C_HEADS * HEAD_DIM), BF16),
        scratch_shapes=[
            pltpu.VMEM((heads, seq, 2 * LANES), BF16),
            pltpu.VMEM((heads, seq, LANES), BF16),
            pltpu.VMEM((LANES, LANES), F32),
            pltpu.VMEM((2, heads * LANES, LANES), BF16),
            pltpu.VMEM((heads, tq, 2 * LANES), BF16),
            pltpu.VMEM((heads, tq, LANES), F32),
            pltpu.VMEM((heads, tq, LANES), F32),
            pltpu.VMEM((2, heads, tq, tq), F32),
        ],
        compiler_params=_cparams(3),
        name="moba_attention",
    )(proj3, proj3, proj3)


def _merge_kernel(x_ref, oa_ref, ob_ref, oc_ref, g_ref, wb_ref, wo_ref, o_ref):
    merged = None
    for n, br in enumerate((oa_ref, ob_ref, oc_ref)):
        y = jnp.dot(br[...], wb_ref[n], preferred_element_type=F32)
        term = jax.nn.sigmoid(g_ref[:, n * D_MODEL:(n + 1) * D_MODEL]) * y
        merged = term if merged is None else merged + term
    o_ref[...] = x_ref[...] + jnp.dot(merged.astype(BF16), wo_ref[...], preferred_element_type=F32)


def _merge(x2, oa, ob, oc, proj2, w_branch, w_out, tm=512):
    t = x2.shape[0]
    assert t % tm == 0
    row = lambda i: (i, 0)
    return pl.pallas_call(
        _merge_kernel,
        grid=(t // tm,),
        in_specs=[
            pl.BlockSpec((tm, D_MODEL), row),
            pl.BlockSpec((tm, BRANCH_WIDTH), row),
            pl.BlockSpec((tm, BRANCH_WIDTH), row),
            pl.BlockSpec((tm, BRANCH_WIDTH), row),
            pl.BlockSpec((tm, N_BRANCH * D_MODEL), lambda i: (i, COL_GATES // (N_BRANCH * D_MODEL))),
            pl.BlockSpec((N_BRANCH, BRANCH_WIDTH, D_MODEL), lambda i: (0, 0, 0)),
            pl.BlockSpec((D_MODEL, D_MODEL), lambda i: (0, 0)),
        ],
        out_specs=pl.BlockSpec((tm, D_MODEL), row),
        out_shape=jax.ShapeDtypeStruct((t, D_MODEL), F32),
        compiler_params=_cparams(1),
        name="gated_merge_out_proj",
    )(x2, oa, ob, oc, proj2, w_branch, w_out)


def _ffn_kernel(x_ref, g_ref, w1_ref, w2_ref, gf_ref, o_ref, h_ref, acc_ref, *, final_norm):
    j = pl.program_id(1)

    @pl.when(j == 0)
    def _():
        x = x_ref[...]
        ms = jnp.mean(x * x, axis=-1, keepdims=True)
        h_ref[...] = (x * lax.rsqrt(ms + NORM_EPS) * g_ref[...]).astype(BF16)
        acc_ref[...] = jnp.zeros(acc_ref.shape, F32)

    u = jnp.maximum(jnp.dot(h_ref[...], w1_ref[...], preferred_element_type=F32), 0.0)
    acc_ref[...] += jnp.dot((u * u).astype(BF16), w2_ref[...], preferred_element_type=F32)

    @pl.when(j == pl.num_programs(1) - 1)
    def _():
        y = x_ref[...] + acc_ref[...]
        if final_norm:
            ms = jnp.mean(y * y, axis=-1, keepdims=True)
            y = y * lax.rsqrt(ms + NORM_EPS) * gf_ref[...]
        o_ref[...] = y


def _ffn(x2, gain, w1, w2, gain_f, final_norm, tm=512, tf=1024):
    t = x2.shape[0]
    assert t % tm == 0 and D_FF % tf == 0
    kern = functools.partial(_ffn_kernel, final_norm=final_norm)
    return pl.pallas_call(
        kern,
        grid=(t // tm, D_FF // tf),
        in_specs=[
            pl.BlockSpec((tm, D_MODEL), lambda i, j: (i, 0)),
            pl.BlockSpec((1, D_MODEL), lambda i, j: (0, 0)),
            pl.BlockSpec((D_MODEL, tf), lambda i, j: (0, j)),
            pl.BlockSpec((tf, D_MODEL), lambda i, j: (j, 0)),
            pl.BlockSpec((1, D_MODEL), lambda i, j: (0, 0)),
        ],
        out_specs=pl.BlockSpec((tm, D_MODEL), lambda i, j: (i, 0)),
        out_shape=jax.ShapeDtypeStruct((t, D_MODEL), F32),
        scratch_shapes=[pltpu.VMEM((tm, D_MODEL), BF16), pltpu.VMEM((tm, D_MODEL), F32)],
        compiler_params=_cparams(2),
        name="relu2_mlp",
    )(x2, gain, w1, w2, gain_f)


def _pack_w_in(w_in):
    sizes = (A_HEADS * HEAD_DIM, HEAD_DIM, HEAD_DIM, IDX_HEADS * IDX_DIM, IDX_DIM, IDX_HEADS,
             B_HEADS * HEAD_DIM, B_KV_HEADS * HEAD_DIM, B_KV_HEADS * HEAD_DIM,
             C_HEADS * HEAD_DIM, C_HEADS * HEAD_DIM, C_HEADS * HEAD_DIM, N_BRANCH * D_MODEL)
    offs = np.concatenate([[0], np.cumsum(sizes)])
    w_in = w_in.astype(BF16)
    aq, ak, av, iq, ik, iw, bq, bk, bv, cq, ck, cv, gates = [
        w_in[..., offs[n]:offs[n + 1]] for n in range(len(sizes))]
    lead = w_in.shape[:-1]
    pad_iw = jnp.zeros(lead + (LANES - HEAD_DIM - IDX_HEADS,), BF16)
    parts = [gates, aq, bq, cq, ck, iq, bk, ak, ik, av, iw, pad_iw, bv, cv]
    used = sum(p.shape[-1] for p in parts)
    parts.append(jnp.zeros(lead + (N_PACKED - used,), BF16))
    return jnp.concatenate(parts, axis=-1)


def _rope_tables(seq):
    half = HEAD_DIM // 2
    inv = 1.0 / (ROPE_THETA ** (jnp.arange(0, HEAD_DIM, 2, dtype=F32) / HEAD_DIM))
    ang = jnp.arange(seq, dtype=F32)[:, None] * inv[None, :]
    cos, sin = jnp.cos(ang), jnp.sin(ang)
    reps = LANES // HEAD_DIM
    cos_t = jnp.tile(jnp.concatenate([cos, cos], axis=1), (1, reps))
    sin_t = jnp.tile(jnp.concatenate([-sin, sin], axis=1), (1, reps))
    assert cos_t.shape == (seq, LANES) and half * 2 == HEAD_DIM
    return cos_t, sin_t


def kernel(x, norm1, w_in, sinks, w_branch, w_out, norm2, w_ff1, w_ff2, norm_f):
    b, seq, d = x.shape
    depth = w_in.shape[0]
    cos_t, sin_t = _rope_tables(seq)
    w_in_p = _pack_w_in(w_in)
    w_branch_b = w_branch.astype(BF16)
    w_out_b = w_out.astype(BF16)
    w_ff1_b = w_ff1.astype(BF16)
    w_ff2_b = w_ff2.astype(BF16)
    gain_f = norm_f.reshape(1, d)

    x2 = x.reshape(b * seq, d)
    for l in range(depth):
        proj2 = _project(x2, norm1[l].reshape(1, d), w_in_p[l], cos_t, sin_t, seq)
        proj3 = proj2.reshape(b, seq, N_PACKED)
        o_a = _dsa(proj3).reshape(b * seq, BRANCH_WIDTH)
        o_b = _swa(proj3, sinks[l]).reshape(b * seq, BRANCH_WIDTH)
        o_c = _moba(proj3).reshape(b * seq, BRANCH_WIDTH)
        x2 = _merge(x2, o_a, o_b, o_c, proj2, w_branch_b[l], w_out_b[l])
        x2 = _ffn(x2, norm2[l].reshape(1, d), w_ff1_b[l], w_ff2_b[l], gain_f,
                  final_norm=(l == depth - 1))
    return x2.reshape(b, seq, d)
```

```python
import functools

import jax
import jax.numpy as jnp
import numpy as np
from jax import lax
from jax.experimental import pallas as pl
from jax.experimental.pallas import tpu as pltpu

F32 = jnp.float32
BF16 = jnp.bfloat16
I32 = jnp.int32

D_MODEL = 1024
HEAD_DIM = 64
ROPE_THETA = 10000.0
NORM_EPS = 1e-6
A_HEADS = 8
IDX_HEADS = 4
IDX_DIM = 64
IDX_TOPK_MAX = 256
B_HEADS = 8
B_KV_HEADS = 2
WINDOW = 128
C_HEADS = 8
MOBA_BLOCK = 256
MOBA_TOPK = 3
Q_BLOCK = 128
N_BRANCH = 3
BRANCH_WIDTH = 8 * HEAD_DIM
D_FF = 4 * D_MODEL

LANES = 128
NEG_BIG = -1e30
Q_SCALE_LOG2 = (HEAD_DIM ** -0.5) * float(np.log2(np.e))
INT_MIN = -(2 ** 31)
BISECT_UNCHECKED_STEPS = 14

COL_GATES = 0
COL_AQ = 3072
COL_BQ = 3584
COL_CQ = 4096
COL_CK = 4608
COL_IQ = 5120
COL_BK = 5376
COL_AKIK = 5504
ROPE_BEGIN, ROPE_END = 3072, 5632
COL_AVIW = 5632
COL_BV = 5760
COL_CV = 5888
N_PACKED = 6656

VMEM_LIMIT = 56 * 1024 * 1024


def _cparams(n_axes):
    return pltpu.CompilerParams(dimension_semantics=("arbitrary",) * n_axes,
                                vmem_limit_bytes=VMEM_LIMIT)


def _nt_dot(a, b):
    return lax.dot_general(a, b, (((1,), (1,)), ((), ())), preferred_element_type=F32)


def _pipelined(n, produce, consume, init):
    produce(0, 0)

    def pair(t, carry):
        c = 2 * t
        produce(c + 1, 1)
        carry = consume(c, 0, carry, False)
        produce(c + 2, 0)
        return consume(c + 1, 1, carry, False)
    n_pairs = (n - 1) // 2
    carry = lax.fori_loop(0, n_pairs, pair, init)
    c0 = 2 * n_pairs

    def two_left(carry):
        produce(c0 + 1, 1)
        carry = consume(c0, 0, carry, False)
        return consume(c0 + 1, 1, carry, True)

    def one_left(carry):
        return consume(c0, 0, carry, True)
    return lax.cond(n - c0 == 2, two_left, one_left, carry)


def _split3_lhs(x):
    hi = x.astype(BF16)
    lo = (x - hi.astype(F32)).astype(BF16)
    return jnp.concatenate([hi, hi, lo], axis=1)


def _split3_rhs(x):
    hi = x.astype(BF16)
    lo = (x - hi.astype(F32)).astype(BF16)
    return jnp.concatenate([hi, lo, hi], axis=1)


def _proj_kernel(x_ref, g_ref, w_ref, cos_ref, sin_ref, o_ref, h_ref, *, rope_lo, rope_hi, tn):
    j = pl.program_id(1)

    @pl.when(j == 0)
    def _():
        x = x_ref[...]
        ms = jnp.mean(x * x, axis=-1, keepdims=True)
        h_ref[...] = (x * lax.rsqrt(ms + NORM_EPS) * g_ref[...]).astype(BF16)

    acc = jnp.dot(h_ref[...], w_ref[...], preferred_element_type=F32)
    is_rope = jnp.logical_and(j >= rope_lo, j < rope_hi)

    @pl.when(is_rope)
    def _():
        cos = cos_ref[...]
        sin = sin_ref[...]
        lane = lax.broadcasted_iota(I32, cos.shape, 1)
        first_half = (lane % HEAD_DIM) < (HEAD_DIM // 2)
        for c in range(tn // LANES):
            a = acc[:, c * LANES:(c + 1) * LANES]
            partner = jnp.where(first_half,
                                pltpu.roll(a, LANES - HEAD_DIM // 2, 1),
                                pltpu.roll(a, HEAD_DIM // 2, 1))
            o_ref[:, c * LANES:(c + 1) * LANES] = a * cos + partner * sin

    @pl.when(jnp.logical_not(is_rope))
    def _():
        o_ref[...] = acc


def _project(x2, gain, w_packed, cos_t, sin_t, seq, tm=1024, tn=512):
    t = x2.shape[0]
    tm = min(tm, seq)
    n = w_packed.shape[1]
    assert t % tm == 0 and seq % tm == 0 and n % tn == 0
    assert ROPE_BEGIN % tn == 0 and ROPE_END % tn == 0
    pos_blocks = seq // tm
    kern = functools.partial(_proj_kernel, rope_lo=ROPE_BEGIN // tn, rope_hi=ROPE_END // tn, tn=tn)
    return pl.pallas_call(
        kern,
        grid=(t // tm, n // tn),
        in_specs=[
            pl.BlockSpec((tm, D_MODEL), lambda i, j: (i, 0)),
            pl.BlockSpec((1, D_MODEL), lambda i, j: (0, 0)),
            pl.BlockSpec((D_MODEL, tn), lambda i, j: (0, j)),
            pl.BlockSpec((tm, LANES), lambda i, j: (i % pos_blocks, 0)),
            pl.BlockSpec((tm, LANES), lambda i, j: (i % pos_blocks, 0)),
        ],
        out_specs=pl.BlockSpec((tm, tn), lambda i, j: (i, j)),
        out_shape=jax.ShapeDtypeStruct((t, n), F32),
        scratch_shapes=[pltpu.VMEM((tm, D_MODEL), BF16)],
        compiler_params=_cparams(2),
        name="norm_proj_rope",
    )(x2, gain, w_packed, cos_t, sin_t)


def _value_with_ones(v):
    lane = lax.broadcasted_iota(I32, v.shape, 1)
    return jnp.concatenate([v.astype(BF16), jnp.where(lane == 0, 1.0, 0.0).astype(BF16)], axis=1)


def _dsa_kernel(aq_ref, iq_ref, kk_ref, vw_ref, o_ref,
                kbf_ref, vbf_ref, ikc_ref, keys_ref, qall_ref, mrow_ref, acc_ref, tri_ref, ones_ref,
                sbuf_ref, *, seq, kc, qb, top_k):
    i = pl.program_id(1)
    n_prep = seq // kc
    idx_scale = (IDX_DIM ** -0.5) * (IDX_HEADS ** -0.5)

    @pl.when(i == 0)
    def _prep():
        def body(c, carry):
            rows = pl.ds(pl.multiple_of(c * kc, kc), kc)
            kk = kk_ref[rows, :]
            kbf_ref[rows, :] = kk[:, :HEAD_DIM].astype(BF16)
            ikc_ref[rows, :] = _split3_rhs(kk[:, HEAD_DIM:])
            vbf_ref[rows, :] = _value_with_ones(vw_ref[rows, :HEAD_DIM])
            return carry
        lax.fori_loop(0, n_prep, body, 0)
        r = lax.broadcasted_iota(I32, (kc, kc), 0)
        cidx = lax.broadcasted_iota(I32, (kc, kc), 1)
        tri_ref[...] = jnp.where(r > cidx, 1.0, 0.0).astype(BF16)
        ones_ref[...] = jnp.ones(ones_ref.shape, BF16)

    nkc = (i * qb) // kc + 1

    iq = iq_ref[...]
    w = vw_ref[pl.ds(pl.multiple_of(i * qb, qb), qb), HEAD_DIM:HEAD_DIM + IDX_HEADS] * idx_scale
    qcat = jnp.concatenate(
        [_split3_lhs(iq[:, h * IDX_DIM:(h + 1) * IDX_DIM]) for h in range(IDX_HEADS)], axis=0)
    row = lax.broadcasted_iota(I32, (qb, LANES), 0)
    lane = lax.broadcasted_iota(I32, (qb, LANES), 1)
    qpos = i * qb + row
    w_b = [jnp.broadcast_to(w[:, h:h + 1], (qb, LANES)) for h in range(IDX_HEADS)]
    ngrp = kc // LANES

    def chunk_rows(c):
        return pl.ds(pl.multiple_of(c * kc, kc), kc)

    def to_key(x):
        bits = lax.bitcast_convert_type(x, I32)
        return jnp.where(bits < 0, bits ^ jnp.int32(0x7FFFFFFF), bits)

    def idx_dots(c, slot):
        sbuf_ref[slot, :IDX_HEADS * qb] = _nt_dot(qcat, ikc_ref[chunk_rows(c), :])

    def score_chunk(c, slot, smax, is_last):
        for g in range(ngrp):
            sc = None
            for h in range(IDX_HEADS):
                d = sbuf_ref[slot, h * qb:(h + 1) * qb, g * LANES:(g + 1) * LANES]
                term = jnp.maximum(d, 0.0) * w_b[h]
                sc = term if sc is None else sc + term
            sc = jnp.where(sc == 0.0, 0.0, sc)
            key = to_key(sc)
            if is_last:
                causal = c * kc + g * LANES + lane <= qpos
                key = jnp.where(causal, key, jnp.int32(INT_MIN))
                sc = jnp.where(causal, sc, -jnp.inf)
            keys_ref[c, :, g * LANES:(g + 1) * LANES] = key
            smax = jnp.maximum(smax, sc)
        return smax

    smax = _pipelined(nkc, idx_dots, score_chunk, jnp.full((qb, LANES), -jnp.inf, F32))
    key_max = to_key(jnp.max(smax, axis=1, keepdims=True))

    def count(pred, thr):
        parts = []
        for r0 in range(0, qb, LANES):
            thr_b = jnp.broadcast_to(thr[r0:r0 + LANES], (LANES, LANES))

            def body(c, acc, r0=r0, thr_b=thr_b):
                kch = keys_ref[c, r0:r0 + LANES, :]
                for g in range(kc // LANES):
                    acc = acc + jnp.where(pred(kch[:, g * LANES:(g + 1) * LANES], thr_b), 1.0, 0.0)
                return acc
            parts.append(lax.fori_loop(0, nkc, body, jnp.zeros((LANES, LANES), F32)))
        return jnp.sum(jnp.concatenate(parts, axis=0), axis=1, keepdims=True)

    ge = lambda k, t: k >= t
    kf = float(top_k)

    def probe(v, state):
        lo, n_lo, hi = state
        n_v = count(ge, v)
        inside = jnp.logical_and(v > lo, v < hi)
        up = jnp.logical_and(inside, n_v >= kf)
        down = jnp.logical_and(inside, n_v < kf)
        return jnp.where(up, v, lo), jnp.where(up, n_v, n_lo), jnp.where(down, v, hi)

    def bisect(state):
        lo, n_lo, hi = state
        mid = lo + lax.shift_right_logical(hi - lo, jnp.int32(1))
        n_mid = count(ge, mid)
        up = n_mid >= kf
        return jnp.where(up, mid, lo), jnp.where(up, n_mid, n_lo), jnp.where(up, hi, mid)

    def unfinished(state):
        lo, n_lo, hi = state
        done = jnp.logical_or(n_lo == kf, hi - 1 == lo)
        return jnp.max(jnp.where(done, 0.0, 1.0)) > 0.0

    state = (jnp.full((qb, 1), INT_MIN, I32), jnp.full((qb, 1), float(seq + 1), F32), key_max + 1)
    state = probe(jnp.ones((qb, 1), I32), state)
    state = probe(jnp.zeros((qb, 1), I32), state)
    two_binades = jnp.int32(1 << 24)
    state = probe(jnp.maximum(key_max, jnp.int32(INT_MIN) + two_binades) - two_binades, state)
    state = lax.fori_loop(0, BISECT_UNCHECKED_STEPS, lambda _, st: bisect(st), state)

    def bisect_cond(carry):
        return jnp.logical_and(carry[0], carry[1] < 34)

    def bisect_body(carry):
        state = bisect(carry[2:])
        return (unfinished(state), carry[1] + 1) + state
    carry = lax.while_loop(bisect_cond, bisect_body, (unfinished(state), jnp.int32(0)) + state)
    thr, n_thr = carry[2], carry[3]

    excess = jnp.where(thr > jnp.int32(INT_MIN), n_thr - kf, float(seq + 1))
    thr_b = jnp.broadcast_to(thr, (qb, LANES))
    excess_b = jnp.broadcast_to(excess, (qb, LANES))

    aq = aq_ref[...]
    for h in range(A_HEADS):
        qall_ref[h * qb:(h + 1) * qb, :] = (
            aq[:, h * HEAD_DIM:(h + 1) * HEAD_DIM] * Q_SCALE_LOG2).astype(BF16)

    def logits(c, slot):
        sbuf_ref[slot] = _nt_dot(qall_ref[...], kbf_ref[chunk_rows(c), :])

    def later_ties(step, slot):
        kch = keys_ref[nkc - 1 - step]
        eq_b = jnp.concatenate(
            [jnp.where(kch[:, g * LANES:(g + 1) * LANES] == thr_b, 1.0, 0.0).astype(BF16)
             for g in range(ngrp)], axis=1)
        sbuf_ref[slot, :qb] = jnp.dot(eq_b, tri_ref[...], preferred_element_type=F32)
        sbuf_ref[slot, qb:2 * qb, :LANES] = jnp.dot(eq_b, ones_ref[...], preferred_element_type=F32)

    def bias_chunk(step, slot, ties_after, is_last):
        c = nkc - 1 - step
        kch = keys_ref[c]
        groups = []
        for g in range(ngrp):
            k_g = kch[:, g * LANES:(g + 1) * LANES]
            later = sbuf_ref[slot, :qb, g * LANES:(g + 1) * LANES] + ties_after
            keep = jnp.logical_or(k_g > thr_b, jnp.logical_and(k_g == thr_b, later >= excess_b))
            groups.append(jnp.where(keep, 0.0, NEG_BIG))
        keys_ref[c] = lax.bitcast_convert_type(jnp.concatenate(groups, axis=1), I32)
        return ties_after + sbuf_ref[slot, qb:2 * qb, :LANES]
    _pipelined(nkc, later_ties, bias_chunk, jnp.zeros((qb, LANES), F32))

    def masked(slot, bias):
        return (sbuf_ref[slot].reshape(A_HEADS, qb, kc) + bias[None]).reshape(A_HEADS * qb, kc)

    def att_chunk(c, slot, carry, is_last):
        s = masked(slot, lax.bitcast_convert_type(keys_ref[c], F32))
        m_old = mrow_ref[...]
        mx = s[:, :LANES]
        for g in range(1, ngrp):
            mx = jnp.maximum(mx, s[:, g * LANES:(g + 1) * LANES])
        m_new = jnp.maximum(m_old, jnp.broadcast_to(jnp.max(mx, axis=1, keepdims=True), m_old.shape))
        p = jnp.concatenate(
            [jnp.exp2(s[:, g * LANES:(g + 1) * LANES] - m_new).astype(BF16) for g in range(ngrp)], axis=1)
        acc_ref[...] = jnp.exp2(m_old - m_new) * acc_ref[...] + jnp.dot(
            p, vbf_ref[chunk_rows(c), :], preferred_element_type=F32)
        mrow_ref[...] = m_new
        return carry
    mrow_ref[...] = jnp.full(mrow_ref.shape, NEG_BIG, F32)
    acc_ref[...] = jnp.zeros(acc_ref.shape, F32)
    _pipelined(nkc, logits, att_chunk, 0)

    acc = acc_ref[...]
    out = acc[:, :HEAD_DIM] / acc[:, HEAD_DIM:HEAD_DIM + 1]
    for h in range(A_HEADS):
        o_ref[:, h * HEAD_DIM:(h + 1) * HEAD_DIM] = out[h * qb:(h + 1) * qb].astype(o_ref.dtype)


def _dsa(proj3, kc=512, qb=256):
    b, seq, _ = proj3.shape
    top_k = min(IDX_TOPK_MAX, seq // 4)
    kc = min(kc, seq)
    assert seq % kc == 0 and kc % qb == 0 and seq % qb == 0 and kc >= top_k
    kern = functools.partial(_dsa_kernel, seq=seq, kc=kc, qb=qb, top_k=top_k)
    hq = A_HEADS * qb
    return pl.pallas_call(
        kern,
        grid=(b, seq // qb),
        in_specs=[
            pl.BlockSpec((None, qb, A_HEADS * HEAD_DIM), lambda bb, i: (bb, i, COL_AQ // 512)),
            pl.BlockSpec((None, qb, IDX_HEADS * IDX_DIM), lambda bb, i: (bb, i, COL_IQ // 256)),
            pl.BlockSpec((None, seq, LANES), lambda bb, i: (bb, 0, COL_AKIK // LANES),
                         pipeline_mode=pl.Buffered(1)),
            pl.BlockSpec((None, seq, LANES), lambda bb, i: (bb, 0, COL_AVIW // LANES),
                         pipeline_mode=pl.Buffered(1)),
        ],
        out_specs=pl.BlockSpec((None, qb, A_HEADS * HEAD_DIM), lambda bb, i: (bb, i, 0)),
        out_shape=jax.ShapeDtypeStruct((b, seq, A_HEADS * HEAD_DIM), BF16),
        scratch_shapes=[
            pltpu.VMEM((seq, HEAD_DIM), BF16),
            pltpu.VMEM((seq, LANES), BF16),
            pltpu.VMEM((seq, 3 * IDX_DIM), BF16),
            pltpu.VMEM((seq // kc, qb, kc), I32),
            pltpu.VMEM((hq, HEAD_DIM), BF16),
            pltpu.VMEM((hq, LANES), F32),
            pltpu.VMEM((hq, LANES), F32),
            pltpu.VMEM((kc, kc), BF16),
            pltpu.VMEM((kc, LANES), BF16),
            pltpu.VMEM((2, hq, kc), F32),
        ],
        compiler_params=_cparams(2),
        name="dsa_topk_attention",
    )(proj3, proj3, proj3, proj3)


def _swa_kernel(sink_ref, q_ref, kp_ref, kc_ref, vp_ref, vc_ref, o_ref):
    i = pl.program_id(1)
    w = WINDOW
    group = B_HEADS // B_KV_HEADS
    q = q_ref[...]
    k = jnp.concatenate([kp_ref[...], kc_ref[...]], axis=0)
    v = jnp.concatenate([vp_ref[...], vc_ref[...]], axis=0)
    srow = lax.broadcasted_iota(I32, (group * w, 2 * w), 0)
    col = lax.broadcasted_iota(I32, (group * w, 2 * w), 1)
    kpos = (i - 1) * w + col
    diff = i * w + (srow % w) - kpos
    mask = jnp.logical_and(jnp.logical_and(diff >= 0, diff < w), kpos >= 0)
    head_of_row = lax.broadcasted_iota(I32, (group * w, 1), 0) // w
    for g in range(B_KV_HEADS):
        kg = k[:, g * HEAD_DIM:(g + 1) * HEAD_DIM].astype(BF16)
        vg = v[:, g * HEAD_DIM:(g + 1) * HEAD_DIM].astype(BF16)
        heads = [g * group + hh for hh in range(group)]
        qg = jnp.concatenate(
            [(q[:, h * HEAD_DIM:(h + 1) * HEAD_DIM] * (HEAD_DIM ** -0.5)).astype(BF16) for h in heads],
            axis=0)
        sink = jnp.zeros((group * w, 1), F32)
        for hh, h in enumerate(heads):
            sink = jnp.where(head_of_row == hh, sink_ref[h], sink)
        s = jnp.where(mask, _nt_dot(qg, kg), NEG_BIG)
        m = jnp.maximum(jnp.max(s, axis=1, keepdims=True), sink)
        p = jnp.exp(s - m)
        den = jnp.sum(p, axis=1, keepdims=True) + jnp.exp(sink - m)
        o = jnp.dot(p.astype(BF16), vg, preferred_element_type=F32) / den
        for hh, h in enumerate(heads):
            o_ref[:, h * HEAD_DIM:(h + 1) * HEAD_DIM] = o[hh * w:(hh + 1) * w].astype(o_ref.dtype)


def _swa(proj3, sinks):
    b, seq, _ = proj3.shape
    w = WINDOW
    kvw = B_KV_HEADS * HEAD_DIM
    assert kvw == LANES and seq % w == 0
    prev = lambda col: (lambda bb, i: (bb, jnp.maximum(i - 1, 0), col))
    cur = lambda col: (lambda bb, i: (bb, i, col))
    return pl.pallas_call(
        _swa_kernel,
        grid=(b, seq // w),
        in_specs=[
            pl.BlockSpec(memory_space=pltpu.SMEM),
            pl.BlockSpec((None, w, B_HEADS * HEAD_DIM), cur(COL_BQ // 512)),
            pl.BlockSpec((None, w, kvw), prev(COL_BK // LANES)),
            pl.BlockSpec((None, w, kvw), cur(COL_BK // LANES)),
            pl.BlockSpec((None, w, kvw), prev(COL_BV // LANES)),
            pl.BlockSpec((None, w, kvw), cur(COL_BV // LANES)),
        ],
        out_specs=pl.BlockSpec((None, w, B_HEADS * HEAD_DIM), lambda bb, i: (bb, i, 0)),
        out_shape=jax.ShapeDtypeStruct((b, seq, B_HEADS * HEAD_DIM), BF16),
        compiler_params=_cparams(2),
        name="swa_sink_attention",
    )(sinks, proj3, proj3, proj3, proj3, proj3)


def _moba_kernel(q_ref, k_ref, v_ref, o_ref,
                 kaug_ref, vaug_ref, kmean_ref, kmw_ref, qaug_ref, mrow_ref, acc_ref, sbuf_ref,
                 *, seq, tq, n_sel):
    i = pl.program_id(2)
    blk = MOBA_BLOCK
    nkb = seq // blk
    heads = LANES // HEAD_DIM
    assert heads == 2
    ones_lane = (HEAD_DIM, 0)

    @pl.when(i == 0)
    def _prep():
        kmean_ref[...] = jnp.zeros(kmean_ref.shape, F32)
        lane_b = lax.broadcasted_iota(I32, (blk, LANES), 1)

        def body(n, carry):
            rows = pl.ds(pl.multiple_of(n * blk, blk), blk)
            kb = k_ref[rows, :]
            vb = v_ref[rows, :]
            onehot = jnp.where(lane_b == n, 1.0, 0.0).astype(BF16)
            for e in range(heads):
                mine = (lane_b // HEAD_DIM) == e
                kaug_ref[e, rows, :] = jnp.concatenate(
                    [onehot, jnp.where(mine, kb, 0.0).astype(BF16)], axis=1)
                ones = jnp.where(lane_b == ones_lane[e], 1.0, 0.0)
                vaug_ref[e, rows, :] = jnp.where(mine, vb, ones).astype(BF16)
            kmean_ref[pl.ds(n, 1), :] = jnp.sum(kb, axis=0, keepdims=True) * (1.0 / blk)
            return carry
        lax.fori_loop(0, nkb, body, 0)
        km = kmean_ref[...]
        lane_m = lax.broadcasted_iota(I32, km.shape, 1)
        kmw = jnp.concatenate(
            [jnp.where((lane_m // HEAD_DIM) == e, km, 0.0) for e in range(heads)], axis=0)
        hi = kmw.astype(BF16)
        kmw_ref[0] = hi
        kmw_ref[1] = (kmw - hi.astype(F32)).astype(BF16)

    q = q_ref[...]
    lane = lax.broadcasted_iota(I32, (tq, LANES), 1)

    qhi = q.astype(BF16)
    qlo = (q - qhi.astype(F32)).astype(BF16)
    gates_t = (_nt_dot(kmw_ref[0], qhi) + _nt_dot(kmw_ref[1], qhi)) + _nt_dot(kmw_ref[0], qlo)
    q2 = (q * Q_SCALE_LOG2).astype(BF16)
    nb8 = -(-nkb // 8) * 8
    own_t = (i * tq + lax.broadcasted_iota(I32, (nb8, tq), 1)) // blk
    blk_t = lax.broadcasted_iota(I32, (nb8, tq), 0)
    blk_tf = blk_t.astype(F32)
    for e in range(heads):
        g = jnp.where(blk_t < own_t, gates_t[e * LANES:e * LANES + nb8], -jnp.inf)
        allowed = blk_t == own_t
        for _ in range(n_sel):
            gmax = jnp.max(g, axis=0, keepdims=True)
            is_max = jnp.logical_and(g == gmax, gmax > -jnp.inf)
            first = jnp.min(jnp.where(is_max, blk_tf, float(LANES)), axis=0, keepdims=True)
            onehot = blk_tf == first
            allowed = jnp.logical_or(allowed, onehot)
            g = jnp.where(onehot, -jnp.inf, g)
        bias_t = jnp.where(allowed, 0.0, NEG_BIG)
        if nb8 < LANES:
            bias_t = jnp.concatenate([bias_t, jnp.full((LANES - nb8, tq), NEG_BIG, F32)], axis=0)
        qaug_ref[e] = jnp.concatenate([bias_t.T.astype(BF16), q2], axis=1)

    kc = tq
    ngrp = kc // LANES
    col = lax.broadcasted_iota(I32, (tq, kc), 1)
    row = lax.broadcasted_iota(I32, (tq, kc), 0)

    def chunk_rows(c):
        return pl.ds(pl.multiple_of(c * kc, kc), kc)

    def logits(c, slot):
        for e in range(heads):
            sbuf_ref[slot, e] = _nt_dot(qaug_ref[e], kaug_ref[e, chunk_rows(c), :])

    def masked(e, slot, diagonal):
        s = sbuf_ref[slot, e]
        return jnp.where(col <= row, s, NEG_BIG) if diagonal else s

    def att_chunk(c, slot, carry, is_last):
        for e in range(heads):
            s = masked(e, slot, is_last)
            m_old = mrow_ref[e]
            mx = s[:, :LANES]
            for gi in range(1, ngrp):
                mx = jnp.maximum(mx, s[:, gi * LANES:(gi + 1) * LANES])
            m_new = jnp.maximum(m_old, jnp.broadcast_to(jnp.max(mx, axis=1, keepdims=True), (tq, LANES)))
            p = jnp.concatenate(
                [jnp.exp2(s[:, gi * LANES:(gi + 1) * LANES] - m_new).astype(BF16) for gi in range(ngrp)],
                axis=1)
            acc_ref[e] = jnp.exp2(m_old - m_new) * acc_ref[e] + jnp.dot(
                p, vaug_ref[e, chunk_rows(c), :], preferred_element_type=F32)
            mrow_ref[e] = m_new
        return carry
    mrow_ref[...] = jnp.full(mrow_ref.shape, NEG_BIG, F32)
    acc_ref[...] = jnp.zeros(acc_ref.shape, F32)
    _pipelined(i + 1, logits, att_chunk, 0)

    outs = []
    for e in range(heads):
        acc = acc_ref[e]
        outs.append(acc / acc[:, ones_lane[e]:ones_lane[e] + 1])
    o_ref[...] = jnp.where(lane < HEAD_DIM, outs[0], outs[1]).astype(o_ref.dtype)


def _moba(proj3, tq=512):
    b, seq, _ = proj3.shape
    tq = min(tq, seq)
    blk = MOBA_BLOCK
    assert seq % blk == 0 and tq % blk == 0 and seq % tq == 0
    nkb = seq // blk
    assert nkb <= LANES
    n_sel = min(MOBA_TOPK, nkb - 1)
    heads = LANES // HEAD_DIM
    kern = functools.partial(_moba_kernel, seq=seq, tq=tq, n_sel=n_sel)
    return pl.pallas_call(
        kern,
        grid=(b, C_HEADS // heads, seq // tq),
        in_specs=[
            pl.BlockSpec((None, tq, LANES), lambda bb, hp, i: (bb, i, COL_CQ // LANES + hp)),
            pl.BlockSpec((None, seq, LANES), lambda bb, hp, i: (bb, 0, COL_CK // LANES + hp)),
            pl.BlockSpec((None, seq, LANES), lambda bb, hp, i: (bb, 0, COL_CV // LANES + hp)),
        ],
        out_specs=pl.BlockSpec((None, tq, LANES), lambda bb, hp, i: (bb, i, hp)),
        out_shape=jax.ShapeDtypeStruct((b, seq, C_HEADS * HEAD_DIM), BF16),
        scratch_shapes=[
            pltpu.VMEM((heads, seq, 2 * LANES), BF16),
            pltpu.VMEM((heads, seq, LANES), BF16),
            pltpu.VMEM((LANES, LANES), F32),
            pltpu.VMEM((2, heads * LANES, LANES), BF16),
            pltpu.VMEM((heads, tq, 2 * LANES), BF16),
            pltpu.VMEM((heads, tq, LANES), F32),
            pltpu.VMEM((heads, tq, LANES), F32),
            pltpu.VMEM((2, heads, tq, tq), F32),
        ],
        compiler_params=_cparams(3),
        name="moba_attention",
    )(proj3, proj3, proj3)


def _merge_kernel(x_ref, oa_ref, ob_ref, oc_ref, g_ref, wb_ref, wo_ref, o_ref):
    merged = None
    for n, br in enumerate((oa_ref, ob_ref, oc_ref)):
        y = jnp.dot(br[...], wb_ref[n], preferred_element_type=F32)
        term = jax.nn.sigmoid(g_ref[:, n * D_MODEL:(n + 1) * D_MODEL]) * y
        merged = term if merged is None else merged + term
    o_ref[...] = x_ref[...] + jnp.dot(merged.astype(BF16), wo_ref[...], preferred_element_type=F32)


def _merge(x2, oa, ob, oc, proj2, w_branch, w_out, tm=512):
    t = x2.shape[0]
    assert t % tm == 0
    row = lambda i: (i, 0)
    return pl.pallas_call(
        _merge_kernel,
        grid=(t // tm,),
        in_specs=[
            pl.BlockSpec((tm, D_MODEL), row),
            pl.BlockSpec((tm, BRANCH_WIDTH), row),
            pl.BlockSpec((tm, BRANCH_WIDTH), row),
            pl.BlockSpec((tm, BRANCH_WIDTH), row),
            pl.BlockSpec((tm, N_BRANCH * D_MODEL), lambda i: (i, COL_GATES // (N_BRANCH * D_MODEL))),
            pl.BlockSpec((N_BRANCH, BRANCH_WIDTH, D_MODEL), lambda i: (0, 0, 0)),
            pl.BlockSpec((D_MODEL, D_MODEL), lambda i: (0, 0)),
        ],
        out_specs=pl.BlockSpec((tm, D_MODEL), row),
        out_shape=jax.ShapeDtypeStruct((t, D_MODEL), F32),
        compiler_params=_cparams(1),
        name="gated_merge_out_proj",
    )(x2, oa, ob, oc, proj2, w_branch, w_out)


def _ffn_kernel(x_ref, g_ref, w1_ref, w2_ref, gf_ref, o_ref, h_ref, acc_ref, *, final_norm):
    j = pl.program_id(1)

    @pl.when(j == 0)
    def _():
        x = x_ref[...]
        ms = jnp.mean(x * x, axis=-1, keepdims=True)
        h_ref[...] = (x * lax.rsqrt(ms + NORM_EPS) * g_ref[...]).astype(BF16)
        acc_ref[...] = jnp.zeros(acc_ref.shape, F32)

    u = jnp.maximum(jnp.dot(h_ref[...], w1_ref[...], preferred_element_type=F32), 0.0)
    acc_ref[...] += jnp.dot((u * u).astype(BF16), w2_ref[...], preferred_element_type=F32)

    @pl.when(j == pl.num_programs(1) - 1)
    def _():
        y = x_ref[...] + acc_ref[...]
        if final_norm:
            ms = jnp.mean(y * y, axis=-1, keepdims=True)
            y = y * lax.rsqrt(ms + NORM_EPS) * gf_ref[...]
        o_ref[...] = y


def _ffn(x2, gain, w1, w2, gain_f, final_norm, tm=1024, tf=1024):
    t = x2.shape[0]
    tm = min(tm, t)
    assert t % tm == 0 and D_FF % tf == 0
    kern = functools.partial(_ffn_kernel, final_norm=final_norm)
    return pl.pallas_call(
        kern,
        grid=(t // tm, D_FF // tf),
        in_specs=[
            pl.BlockSpec((tm, D_MODEL), lambda i, j: (i, 0)),
            pl.BlockSpec((1, D_MODEL), lambda i, j: (0, 0)),
            pl.BlockSpec((D_MODEL, tf), lambda i, j: (0, j)),
            pl.BlockSpec((tf, D_MODEL), lambda i, j: (j, 0)),
            pl.BlockSpec((1, D_MODEL), lambda i, j: (0, 0)),
        ],
        out_specs=pl.BlockSpec((tm, D_MODEL), lambda i, j: (i, 0)),
        out_shape=jax.ShapeDtypeStruct((t, D_MODEL), F32),
        scratch_shapes=[pltpu.VMEM((tm, D_MODEL), BF16), pltpu.VMEM((tm, D_MODEL), F32)],
        compiler_params=_cparams(2),
        name="relu2_mlp",
    )(x2, gain, w1, w2, gain_f)


def _pack_w_in(w_in):
    sizes = (A_HEADS * HEAD_DIM, HEAD_DIM, HEAD_DIM, IDX_HEADS * IDX_DIM, IDX_DIM, IDX_HEADS,
             B_HEADS * HEAD_DIM, B_KV_HEADS * HEAD_DIM, B_KV_HEADS * HEAD_DIM,
             C_HEADS * HEAD_DIM, C_HEADS * HEAD_DIM, C_HEADS * HEAD_DIM, N_BRANCH * D_MODEL)
    offs = np.concatenate([[0], np.cumsum(sizes)])
    w_in = w_in.astype(BF16)
    aq, ak, av, iq, ik, iw, bq, bk, bv, cq, ck, cv, gates = [
        w_in[..., offs[n]:offs[n + 1]] for n in range(len(sizes))]
    lead = w_in.shape[:-1]
    pad_iw = jnp.zeros(lead + (LANES - HEAD_DIM - IDX_HEADS,), BF16)
    parts = [gates, aq, bq, cq, ck, iq, bk, ak, ik, av, iw, pad_iw, bv, cv]
    used = sum(p.shape[-1] for p in parts)
    parts.append(jnp.zeros(lead + (N_PACKED - used,), BF16))
    return jnp.concatenate(parts, axis=-1)


def _rope_tables(seq):
    half = HEAD_DIM // 2
    inv = 1.0 / (ROPE_THETA ** (jnp.arange(0, HEAD_DIM, 2, dtype=F32) / HEAD_DIM))
    ang = jnp.arange(seq, dtype=F32)[:, None] * inv[None, :]
    cos, sin = jnp.cos(ang), jnp.sin(ang)
    reps = LANES // HEAD_DIM
    cos_t = jnp.tile(jnp.concatenate([cos, cos], axis=1), (1, reps))
    sin_t = jnp.tile(jnp.concatenate([-sin, sin], axis=1), (1, reps))
    assert cos_t.shape == (seq, LANES) and half * 2 == HEAD_DIM
    return cos_t, sin_t


def kernel(x, norm1, w_in, sinks, w_branch, w_out, norm2, w_ff1, w_ff2, norm_f):
    b, seq, d = x.shape
    depth = w_in.shape[0]
    cos_t, sin_t = _rope_tables(seq)
    w_in_p = _pack_w_in(w_in)
    w_branch_b = w_branch.astype(BF16)
    w_out_b = w_out.astype(BF16)
    w_ff1_b = w_ff1.astype(BF16)
    w_ff2_b = w_ff2.astype(BF16)
    gain_f = norm_f.reshape(1, d)

    x2 = x.reshape(b * seq, d)
    for l in range(depth):
        proj2 = _project(x2, norm1[l].reshape(1, d), w_in_p[l], cos_t, sin_t, seq)
        proj3 = proj2.reshape(b, seq, N_PACKED)
        o_a = _dsa(proj3).reshape(b * seq, BRANCH_WIDTH)
        o_b = _swa(proj3, sinks[l]).reshape(b * seq, BRANCH_WIDTH)
        o_c = _moba(proj3).reshape(b * seq, BRANCH_WIDTH)
        x2 = _merge(x2, o_a, o_b, o_c, proj2, w_branch_b[l], w_out_b[l])
        x2 = _ffn(x2, norm2[l].reshape(1, d), w_ff1_b[l], w_ff2_b[l], gain_f,
                  final_norm=(l == depth - 1))
    return x2.reshape(b, seq, d)
```

```python
import functools

import jax
import jax.numpy as jnp
import numpy as np
from jax import lax
from jax.experimental import pallas as pl
from jax.experimental.pallas import tpu as pltpu

F32 = jnp.float32
BF16 = jnp.bfloat16
I32 = jnp.int32

D_MODEL = 1024
HEAD_DIM = 64
ROPE_THETA = 10000.0
NORM_EPS = 1e-6
A_HEADS = 8
IDX_HEADS = 4
IDX_DIM = 64
IDX_TOPK_MAX = 256
B_HEADS = 8
B_KV_HEADS = 2
WINDOW = 128
C_HEADS = 8
MOBA_BLOCK = 256
MOBA_TOPK = 3
Q_BLOCK = 128
N_BRANCH = 3
BRANCH_WIDTH = 8 * HEAD_DIM
D_FF = 4 * D_MODEL

LANES = 128
NEG_BIG = -1e30
Q_SCALE_LOG2 = (HEAD_DIM ** -0.5) * float(np.log2(np.e))
INT_MIN = -(2 ** 31)
BISECT_UNCHECKED_STEPS = 14

COL_GATES = 0
COL_AQ = 3072
COL_BQ = 3584
COL_CQ = 4096
COL_CK = 4608
COL_IQ = 5120
COL_BK = 5376
COL_AKIK = 5504
ROPE_BEGIN, ROPE_END = 3072, 5632
COL_AVIW = 5632
COL_BV = 5760
COL_CV = 5888
N_PACKED = 6656

VMEM_LIMIT = 56 * 1024 * 1024


def _cparams(n_axes):
    return pltpu.CompilerParams(dimension_semantics=("arbitrary",) * n_axes,
                                vmem_limit_bytes=VMEM_LIMIT)


def _nt_dot(a, b):
    return lax.dot_general(a, b, (((1,), (1,)), ((), ())), preferred_element_type=F32)


def _pipelined(n, produce, consume, init):
    produce(0, 0)

    def pair(t, carry):
        c = 2 * t
        produce(c + 1, 1)
        carry = consume(c, 0, carry, False)
        produce(c + 2, 0)
        return consume(c + 1, 1, carry, False)
    n_pairs = (n - 1) // 2
    carry = lax.fori_loop(0, n_pairs, pair, init)
    c0 = 2 * n_pairs

    def two_left(carry):
        produce(c0 + 1, 1)
        carry = consume(c0, 0, carry, False)
        return consume(c0 + 1, 1, carry, True)

    def one_left(carry):
        return consume(c0, 0, carry, True)
    return lax.cond(n - c0 == 2, two_left, one_left, carry)


def _split3_lhs(x):
    hi = x.astype(BF16)
    lo = (x - hi.astype(F32)).astype(BF16)
    return jnp.concatenate([hi, hi, lo], axis=1)


def _split3_rhs(x):
    hi = x.astype(BF16)
    lo = (x - hi.astype(F32)).astype(BF16)
    return jnp.concatenate([hi, lo, hi], axis=1)


def _proj_kernel(x_ref, g_ref, w_ref, cos_ref, sin_ref, o_ref, h_ref, *, rope_lo, rope_hi, tn):
    j = pl.program_id(1)

    @pl.when(j == 0)
    def _():
        x = x_ref[...]
        ms = jnp.mean(x * x, axis=-1, keepdims=True)
        h_ref[...] = (x * lax.rsqrt(ms + NORM_EPS) * g_ref[...]).astype(BF16)

    acc = jnp.dot(h_ref[...], w_ref[...], preferred_element_type=F32)
    is_rope = jnp.logical_and(j >= rope_lo, j < rope_hi)

    @pl.when(is_rope)
    def _():
        cos = cos_ref[...]
        sin = sin_ref[...]
        lane = lax.broadcasted_iota(I32, cos.shape, 1)
        first_half = (lane % HEAD_DIM) < (HEAD_DIM // 2)
        for c in range(tn // LANES):
            a = acc[:, c * LANES:(c + 1) * LANES]
            partner = jnp.where(first_half,
                                pltpu.roll(a, LANES - HEAD_DIM // 2, 1),
                                pltpu.roll(a, HEAD_DIM // 2, 1))
            o_ref[:, c * LANES:(c + 1) * LANES] = a * cos + partner * sin

    @pl.when(jnp.logical_not(is_rope))
    def _():
        o_ref[...] = acc


def _project(x2, gain, w_packed, cos_t, sin_t, seq, tm=2048, tn=512):
    t = x2.shape[0]
    tm = min(tm, seq)
    n = w_packed.shape[1]
    assert t % tm == 0 and seq % tm == 0 and n % tn == 0
    assert ROPE_BEGIN % tn == 0 and ROPE_END % tn == 0
    pos_blocks = seq // tm
    kern = functools.partial(_proj_kernel, rope_lo=ROPE_BEGIN // tn, rope_hi=ROPE_END // tn, tn=tn)
    return pl.pallas_call(
        kern,
        grid=(t // tm, n // tn),
        in_specs=[
            pl.BlockSpec((tm, D_MODEL), lambda i, j: (i, 0)),
            pl.BlockSpec((1, D_MODEL), lambda i, j: (0, 0)),
            pl.BlockSpec((D_MODEL, tn), lambda i, j: (0, j)),
            pl.BlockSpec((tm, LANES), lambda i, j: (i % pos_blocks, 0)),
            pl.BlockSpec((tm, LANES), lambda i, j: (i % pos_blocks, 0)),
        ],
        out_specs=pl.BlockSpec((tm, tn), lambda i, j: (i, j)),
        out_shape=jax.ShapeDtypeStruct((t, n), F32),
        scratch_shapes=[pltpu.VMEM((tm, D_MODEL), BF16)],
        compiler_params=_cparams(2),
        name="norm_proj_rope",
    )(x2, gain, w_packed, cos_t, sin_t)


def _value_with_ones(v):
    lane = lax.broadcasted_iota(I32, v.shape, 1)
    return jnp.concatenate([v.astype(BF16), jnp.where(lane == 0, 1.0, 0.0).astype(BF16)], axis=1)


def _dsa_kernel(aq_ref, iq_ref, kk_ref, vw_ref, o_ref,
                kbf_ref, vbf_ref, ikc_ref, keys_ref, qall_ref, mrow_ref, acc_ref, tri_ref, ones_ref,
                sbuf_ref, *, seq, kc, qb, top_k):
    i = pl.program_id(1)
    n_prep = seq // kc
    idx_scale = (IDX_DIM ** -0.5) * (IDX_HEADS ** -0.5)

    @pl.when(i == 0)
    def _prep():
        def body(c, carry):
            rows = pl.ds(pl.multiple_of(c * kc, kc), kc)
            kk = kk_ref[rows, :]
            kbf_ref[rows, :] = kk[:, :HEAD_DIM].astype(BF16)
            ikc_ref[rows, :] = _split3_rhs(kk[:, HEAD_DIM:])
            vbf_ref[rows, :] = _value_with_ones(vw_ref[rows, :HEAD_DIM])
            return carry
        lax.fori_loop(0, n_prep, body, 0)
        r = lax.broadcasted_iota(I32, (kc, kc), 0)
        cidx = lax.broadcasted_iota(I32, (kc, kc), 1)
        tri_ref[...] = jnp.where(r > cidx, 1.0, 0.0).astype(BF16)
        ones_ref[...] = jnp.ones(ones_ref.shape, BF16)

    nkc = (i * qb) // kc + 1

    iq = iq_ref[...]
    w = vw_ref[pl.ds(pl.multiple_of(i * qb, qb), qb), HEAD_DIM:HEAD_DIM + IDX_HEADS] * idx_scale
    qcat = jnp.concatenate(
        [_split3_lhs(iq[:, h * IDX_DIM:(h + 1) * IDX_DIM]) for h in range(IDX_HEADS)], axis=0)
    row = lax.broadcasted_iota(I32, (qb, LANES), 0)
    lane = lax.broadcasted_iota(I32, (qb, LANES), 1)
    qpos = i * qb + row
    w_b = [jnp.broadcast_to(w[:, h:h + 1], (qb, LANES)) for h in range(IDX_HEADS)]
    ngrp = kc // LANES

    def chunk_rows(c):
        return pl.ds(pl.multiple_of(c * kc, kc), kc)

    def to_key(x):
        bits = lax.bitcast_convert_type(x, I32)
        return jnp.where(bits < 0, bits ^ jnp.int32(0x7FFFFFFF), bits)

    def idx_dots(c, slot):
        sbuf_ref[slot, :IDX_HEADS * qb] = _nt_dot(qcat, ikc_ref[chunk_rows(c), :])

    def score_chunk(c, slot, smax, is_last):
        for g in range(ngrp):
            sc = None
            for h in range(IDX_HEADS):
                d = sbuf_ref[slot, h * qb:(h + 1) * qb, g * LANES:(g + 1) * LANES]
                term = jnp.maximum(d, 0.0) * w_b[h]
                sc = term if sc is None else sc + term
            sc = jnp.where(sc == 0.0, 0.0, sc)
            key = to_key(sc)
            if is_last:
                causal = c * kc + g * LANES + lane <= qpos
                key = jnp.where(causal, key, jnp.int32(INT_MIN))
                sc = jnp.where(causal, sc, -jnp.inf)
            keys_ref[c, :, g * LANES:(g + 1) * LANES] = key
            smax = jnp.maximum(smax, sc)
        return smax

    smax = _pipelined(nkc, idx_dots, score_chunk, jnp.full((qb, LANES), -jnp.inf, F32))
    key_max = to_key(jnp.max(smax, axis=1, keepdims=True))

    def count(pred, thr):
        parts = []
        for r0 in range(0, qb, LANES):
            thr_b = jnp.broadcast_to(thr[r0:r0 + LANES], (LANES, LANES))

            def body(c, acc, r0=r0, thr_b=thr_b):
                kch = keys_ref[c, r0:r0 + LANES, :]
                for g in range(kc // LANES):
                    acc = acc + jnp.where(pred(kch[:, g * LANES:(g + 1) * LANES], thr_b), 1.0, 0.0)
                return acc
            parts.append(lax.fori_loop(0, nkc, body, jnp.zeros((LANES, LANES), F32)))
        return jnp.sum(jnp.concatenate(parts, axis=0), axis=1, keepdims=True)

    ge = lambda k, t: k >= t
    kf = float(top_k)

    def probe(v, state):
        lo, n_lo, hi = state
        n_v = count(ge, v)
        inside = jnp.logical_and(v > lo, v < hi)
        up = jnp.logical_and(inside, n_v >= kf)
        down = jnp.logical_and(inside, n_v < kf)
        return jnp.where(up, v, lo), jnp.where(up, n_v, n_lo), jnp.where(down, v, hi)

    def bisect(state):
        lo, n_lo, hi = state
        mid = lo + lax.shift_right_logical(hi - lo, jnp.int32(1))
        n_mid = count(ge, mid)
        up = n_mid >= kf
        return jnp.where(up, mid, lo), jnp.where(up, n_mid, n_lo), jnp.where(up, hi, mid)

    def unfinished(state):
        lo, n_lo, hi = state
        done = jnp.logical_or(n_lo == kf, hi - 1 == lo)
        return jnp.max(jnp.where(done, 0.0, 1.0)) > 0.0

    state = (jnp.full((qb, 1), INT_MIN, I32), jnp.full((qb, 1), float(seq + 1), F32), key_max + 1)
    state = probe(jnp.ones((qb, 1), I32), state)
    state = probe(jnp.zeros((qb, 1), I32), state)
    two_binades = jnp.int32(1 << 24)
    state = probe(jnp.maximum(key_max, jnp.int32(INT_MIN) + two_binades) - two_binades, state)
    state = lax.fori_loop(0, BISECT_UNCHECKED_STEPS, lambda _, st: bisect(st), state)

    def bisect_cond(carry):
        return jnp.logical_and(carry[0], carry[1] < 34)

    def bisect_body(carry):
        state = bisect(carry[2:])
        return (unfinished(state), carry[1] + 1) + state
    carry = lax.while_loop(bisect_cond, bisect_body, (unfinished(state), jnp.int32(0)) + state)
    thr, n_thr = carry[2], carry[3]

    excess = jnp.where(thr > jnp.int32(INT_MIN), n_thr - kf, float(seq + 1))
    thr_b = jnp.broadcast_to(thr, (qb, LANES))
    excess_b = jnp.broadcast_to(excess, (qb, LANES))

    aq = aq_ref[...]
    for h in range(A_HEADS):
        qall_ref[h * qb:(h + 1) * qb, :] = (
            aq[:, h * HEAD_DIM:(h + 1) * HEAD_DIM] * Q_SCALE_LOG2).astype(BF16)

    def logits(c, slot):
        sbuf_ref[slot] = _nt_dot(qall_ref[...], kbf_ref[chunk_rows(c), :])

    def later_ties(step, slot):
        kch = keys_ref[nkc - 1 - step]
        eq_b = jnp.concatenate(
            [jnp.where(kch[:, g * LANES:(g + 1) * LANES] == thr_b, 1.0, 0.0).astype(BF16)
             for g in range(ngrp)], axis=1)
        sbuf_ref[slot, :qb] = jnp.dot(eq_b, tri_ref[...], preferred_element_type=F32)
        sbuf_ref[slot, qb:2 * qb, :LANES] = jnp.dot(eq_b, ones_ref[...], preferred_element_type=F32)

    def bias_chunk(step, slot, ties_after, is_last):
        c = nkc - 1 - step
        kch = keys_ref[c]
        groups = []
        for g in range(ngrp):
            k_g = kch[:, g * LANES:(g + 1) * LANES]
            later = sbuf_ref[slot, :qb, g * LANES:(g + 1) * LANES] + ties_after
            keep = jnp.logical_or(k_g > thr_b, jnp.logical_and(k_g == thr_b, later >= excess_b))
            groups.append(jnp.where(keep, 0.0, NEG_BIG))
        keys_ref[c] = lax.bitcast_convert_type(jnp.concatenate(groups, axis=1), I32)
        return ties_after + sbuf_ref[slot, qb:2 * qb, :LANES]
    _pipelined(nkc, later_ties, bias_chunk, jnp.zeros((qb, LANES), F32))

    def masked(slot, bias):
        return (sbuf_ref[slot].reshape(A_HEADS, qb, kc) + bias[None]).reshape(A_HEADS * qb, kc)

    def att_chunk(c, slot, carry, is_last):
        s = masked(slot, lax.bitcast_convert_type(keys_ref[c], F32))
        m_old = mrow_ref[...]
        mx = s[:, :LANES]
        for g in range(1, ngrp):
            mx = jnp.maximum(mx, s[:, g * LANES:(g + 1) * LANES])
        m_new = jnp.maximum(m_old, jnp.broadcast_to(jnp.max(mx, axis=1, keepdims=True), m_old.shape))
        p = jnp.concatenate(
            [jnp.exp2(s[:, g * LANES:(g + 1) * LANES] - m_new).astype(BF16) for g in range(ngrp)], axis=1)
        acc_ref[...] = jnp.exp2(m_old - m_new) * acc_ref[...] + jnp.dot(
            p, vbf_ref[chunk_rows(c), :], preferred_element_type=F32)
        mrow_ref[...] = m_new
        return carry
    mrow_ref[...] = jnp.full(mrow_ref.shape, NEG_BIG, F32)
    acc_ref[...] = jnp.zeros(acc_ref.shape, F32)
    _pipelined(nkc, logits, att_chunk, 0)

    acc = acc_ref[...]
    out = acc[:, :HEAD_DIM] / acc[:, HEAD_DIM:HEAD_DIM + 1]
    for h in range(A_HEADS):
        o_ref[:, h * HEAD_DIM:(h + 1) * HEAD_DIM] = out[h * qb:(h + 1) * qb].astype(o_ref.dtype)


def _dsa(proj3, kc=512, qb=256):
    b, seq, _ = proj3.shape
    top_k = min(IDX_TOPK_MAX, seq // 4)
    kc = min(kc, seq)
    assert seq % kc == 0 and kc % qb == 0 and seq % qb == 0 and kc >= top_k
    kern = functools.partial(_dsa_kernel, seq=seq, kc=kc, qb=qb, top_k=top_k)
    hq = A_HEADS * qb
    return pl.pallas_call(
        kern,
        grid=(b, seq // qb),
        in_specs=[
            pl.BlockSpec((None, qb, A_HEADS * HEAD_DIM), lambda bb, i: (bb, i, COL_AQ // 512)),
            pl.BlockSpec((None, qb, IDX_HEADS * IDX_DIM), lambda bb, i: (bb, i, COL_IQ // 256)),
            pl.BlockSpec((None, seq, LANES), lambda bb, i: (bb, 0, COL_AKIK // LANES),
                         pipeline_mode=pl.Buffered(1)),
            pl.BlockSpec((None, seq, LANES), lambda bb, i: (bb, 0, COL_AVIW // LANES),
                         pipeline_mode=pl.Buffered(1)),
        ],
        out_specs=pl.BlockSpec((None, qb, A_HEADS * HEAD_DIM), lambda bb, i: (bb, i, 0)),
        out_shape=jax.ShapeDtypeStruct((b, seq, A_HEADS * HEAD_DIM), BF16),
        scratch_shapes=[
            pltpu.VMEM((seq, HEAD_DIM), BF16),
            pltpu.VMEM((seq, LANES), BF16),
            pltpu.VMEM((seq, 3 * IDX_DIM), BF16),
            pltpu.VMEM((seq // kc, qb, kc), I32),
            pltpu.VMEM((hq, HEAD_DIM), BF16),
            pltpu.VMEM((hq, LANES), F32),
            pltpu.VMEM((hq, LANES), F32),
            pltpu.VMEM((kc, kc), BF16),
            pltpu.VMEM((kc, LANES), BF16),
            pltpu.VMEM((2, hq, kc), F32),
        ],
        compiler_params=_cparams(2),
        name="dsa_topk_attention",
    )(proj3, proj3, proj3, proj3)


def _swa_kernel(sink_ref, q_ref, kp_ref, kc_ref, vp_ref, vc_ref, o_ref):
    i = pl.program_id(1)
    w = WINDOW
    group = B_HEADS // B_KV_HEADS
    q = q_ref[...]
    k = jnp.concatenate([kp_ref[...], kc_ref[...]], axis=0)
    v = jnp.concatenate([vp_ref[...], vc_ref[...]], axis=0)
    srow = lax.broadcasted_iota(I32, (group * w, 2 * w), 0)
    col = lax.broadcasted_iota(I32, (group * w, 2 * w), 1)
    kpos = (i - 1) * w + col
    diff = i * w + (srow % w) - kpos
    mask = jnp.logical_and(jnp.logical_and(diff >= 0, diff < w), kpos >= 0)
    head_of_row = lax.broadcasted_iota(I32, (group * w, 1), 0) // w
    for g in range(B_KV_HEADS):
        kg = k[:, g * HEAD_DIM:(g + 1) * HEAD_DIM].astype(BF16)
        vg = v[:, g * HEAD_DIM:(g + 1) * HEAD_DIM].astype(BF16)
        heads = [g * group + hh for hh in range(group)]
        qg = jnp.concatenate(
            [(q[:, h * HEAD_DIM:(h + 1) * HEAD_DIM] * (HEAD_DIM ** -0.5)).astype(BF16) for h in heads],
            axis=0)
        sink = jnp.zeros((group * w, 1), F32)
        for hh, h in enumerate(heads):
            sink = jnp.where(head_of_row == hh, sink_ref[h], sink)
        s = jnp.where(mask, _nt_dot(qg, kg), NEG_BIG)
        m = jnp.maximum(jnp.max(s, axis=1, keepdims=True), sink)
        p = jnp.exp(s - m)
        den = jnp.sum(p, axis=1, keepdims=True) + jnp.exp(sink - m)
        o = jnp.dot(p.astype(BF16), vg, preferred_element_type=F32) / den
        for hh, h in enumerate(heads):
            o_ref[:, h * HEAD_DIM:(h + 1) * HEAD_DIM] = o[hh * w:(hh + 1) * w].astype(o_ref.dtype)


def _swa(proj3, sinks):
    b, seq, _ = proj3.shape
    w = WINDOW
    kvw = B_KV_HEADS * HEAD_DIM
    assert kvw == LANES and seq % w == 0
    prev = lambda col: (lambda bb, i: (bb, jnp.maximum(i - 1, 0), col))
    cur = lambda col: (lambda bb, i: (bb, i, col))
    return pl.pallas_call(
        _swa_kernel,
        grid=(b, seq // w),
        in_specs=[
            pl.BlockSpec(memory_space=pltpu.SMEM),
            pl.BlockSpec((None, w, B_HEADS * HEAD_DIM), cur(COL_BQ // 512)),
            pl.BlockSpec((None, w, kvw), prev(COL_BK // LANES)),
            pl.BlockSpec((None, w, kvw), cur(COL_BK // LANES)),
            pl.BlockSpec((None, w, kvw), prev(COL_BV // LANES)),
            pl.BlockSpec((None, w, kvw), cur(COL_BV // LANES)),
        ],
        out_specs=pl.BlockSpec((None, w, B_HEADS * HEAD_DIM), lambda bb, i: (bb, i, 0)),
        out_shape=jax.ShapeDtypeStruct((b, seq, B_HEADS * HEAD_DIM), BF16),
        compiler_params=_cparams(2),
        name="swa_sink_attention",
    )(sinks, proj3, proj3, proj3, proj3, proj3)


def _moba_kernel(q_ref, k_ref, v_ref, o_ref,
                 kaug_ref, vaug_ref, kmean_ref, kmw_ref, qaug_ref, mrow_ref, acc_ref, sbuf_ref,
                 *, seq, tq, n_sel):
    i = pl.program_id(2)
    blk = MOBA_BLOCK
    nkb = seq // blk
    heads = LANES // HEAD_DIM
    assert heads == 2
    ones_lane = (HEAD_DIM, 0)

    @pl.when(i == 0)
    def _prep():
        kmean_ref[...] = jnp.zeros(kmean_ref.shape, F32)
        lane_b = lax.broadcasted_iota(I32, (blk, LANES), 1)

        def body(n, carry):
            rows = pl.ds(pl.multiple_of(n * blk, blk), blk)
            kb = k_ref[rows, :]
            vb = v_ref[rows, :]
            onehot = jnp.where(lane_b == n, 1.0, 0.0).astype(BF16)
            for e in range(heads):
                mine = (lane_b // HEAD_DIM) == e
                kaug_ref[e, rows, :] = jnp.concatenate(
                    [onehot, jnp.where(mine, kb, 0.0).astype(BF16)], axis=1)
                ones = jnp.where(lane_b == ones_lane[e], 1.0, 0.0)
                vaug_ref[e, rows, :] = jnp.where(mine, vb, ones).astype(BF16)
            kmean_ref[pl.ds(n, 1), :] = jnp.sum(kb, axis=0, keepdims=True) * (1.0 / blk)
            return carry
        lax.fori_loop(0, nkb, body, 0)
        km = kmean_ref[...]
        lane_m = lax.broadcasted_iota(I32, km.shape, 1)
        kmw = jnp.concatenate(
            [jnp.where((lane_m // HEAD_DIM) == e, km, 0.0) for e in range(heads)], axis=0)
        hi = kmw.astype(BF16)
        kmw_ref[0] = hi
        kmw_ref[1] = (kmw - hi.astype(F32)).astype(BF16)

    q = q_ref[...]
    lane = lax.broadcasted_iota(I32, (tq, LANES), 1)

    qhi = q.astype(BF16)
    qlo = (q - qhi.astype(F32)).astype(BF16)
    gates_t = (_nt_dot(kmw_ref[0], qhi) + _nt_dot(kmw_ref[1], qhi)) + _nt_dot(kmw_ref[0], qlo)
    q2 = (q * Q_SCALE_LOG2).astype(BF16)
    nb8 = -(-nkb // 8) * 8
    own_t = (i * tq + lax.broadcasted_iota(I32, (nb8, tq), 1)) // blk
    blk_t = lax.broadcasted_iota(I32, (nb8, tq), 0)
    blk_tf = blk_t.astype(F32)
    for e in range(heads):
        g = jnp.where(blk_t < own_t, gates_t[e * LANES:e * LANES + nb8], -jnp.inf)
        allowed = blk_t == own_t
        for _ in range(n_sel):
            gmax = jnp.max(g, axis=0, keepdims=True)
            is_max = jnp.logical_and(g == gmax, gmax > -jnp.inf)
            first = jnp.min(jnp.where(is_max, blk_tf, float(LANES)), axis=0, keepdims=True)
            onehot = blk_tf == first
            allowed = jnp.logical_or(allowed, onehot)
            g = jnp.where(onehot, -jnp.inf, g)
        bias_t = jnp.where(allowed, 0.0, NEG_BIG)
        if nb8 < LANES:
            bias_t = jnp.concatenate([bias_t, jnp.full((LANES - nb8, tq), NEG_BIG, F32)], axis=0)
        qaug_ref[e] = jnp.concatenate([bias_t.T.astype(BF16), q2], axis=1)

    kc = tq
    ngrp = kc // LANES
    col = lax.broadcasted_iota(I32, (tq, kc), 1)
    row = lax.broadcasted_iota(I32, (tq, kc), 0)

    def chunk_rows(c):
        return pl.ds(pl.multiple_of(c * kc, kc), kc)

    def logits(c, slot):
        for e in range(heads):
            sbuf_ref[slot, e] = _nt_dot(qaug_ref[e], kaug_ref[e, chunk_rows(c), :])

    def masked(e, slot, diagonal):
        s = sbuf_ref[slot, e]
        return jnp.where(col <= row, s, NEG_BIG) if diagonal else s

    def att_chunk(c, slot, carry, is_last):
        for e in range(heads):
            s = masked(e, slot, is_last)
            m_old = mrow_ref[e]
            mx = s[:, :LANES]
            for gi in range(1, ngrp):
                mx = jnp.maximum(mx, s[:, gi * LANES:(gi + 1) * LANES])
            m_new = jnp.maximum(m_old, jnp.broadcast_to(jnp.max(mx, axis=1, keepdims=True), (tq, LANES)))
            p = jnp.concatenate(
                [jnp.exp2(s[:, gi * LANES:(gi + 1) * LANES] - m_new).astype(BF16) for gi in range(ngrp)],
                axis=1)
            acc_ref[e] = jnp.exp2(m_old - m_new) * acc_ref[e] + jnp.dot(
                p, vaug_ref[e, chunk_rows(c), :], preferred_element_type=F32)
            mrow_ref[e] = m_new
        return carry
    mrow_ref[...] = jnp.full(mrow_ref.shape, NEG_BIG, F32)
    acc_ref[...] = jnp.zeros(acc_ref.shape, F32)
    _pipelined(i + 1, logits, att_chunk, 0)

    outs = []
    for e in range(heads):
        acc = acc_ref[e]
        outs.append(acc / acc[:, ones_lane[e]:ones_lane[e] + 1])
    o_ref[...] = jnp.where(lane < HEAD_DIM, outs[0], outs[1]).astype(o_ref.dtype)


def _moba(proj3, tq=512):
    b, seq, _ = proj3.shape
    tq = min(tq, seq)
    blk = MOBA_BLOCK
    assert seq % blk == 0 and tq % blk == 0 and seq % tq == 0
    nkb = seq // blk
    assert nkb <= LANES
    n_sel = min(MOBA_TOPK, nkb - 1)
    heads = LANES // HEAD_DIM
    kern = functools.partial(_moba_kernel, seq=seq, tq=tq, n_sel=n_sel)
    return pl.pallas_call(
        kern,
        grid=(b, C_HEADS // heads, seq // tq),
        in_specs=[
            pl.BlockSpec((None, tq, LANES), lambda bb, hp, i: (bb, i, COL_CQ // LANES + hp)),
            pl.BlockSpec((None, seq, LANES), lambda bb, hp, i: (bb, 0, COL_CK // LANES + hp)),
            pl.BlockSpec((None, seq, LANES), lambda bb, hp, i: (bb, 0, COL_CV // LANES + hp)),
        ],
        out_specs=pl.BlockSpec((None, tq, LANES), lambda bb, hp, i: (bb, i, hp)),
        out_shape=jax.ShapeDtypeStruct((b, seq, C_HEADS * HEAD_DIM), BF16),
        scratch_shapes=[
            pltpu.VMEM((heads, seq, 2 * LANES), BF16),
            pltpu.VMEM((heads, seq, LANES), BF16),
            pltpu.VMEM((LANES, LANES), F32),
            pltpu.VMEM((2, heads * LANES, LANES), BF16),
            pltpu.VMEM((heads, tq, 2 * LANES), BF16),
            pltpu.VMEM((heads, tq, LANES), F32),
            pltpu.VMEM((heads, tq, LANES), F32),
            pltpu.VMEM((2, heads, tq, tq), F32),
        ],
        compiler_params=_cparams(3),
        name="moba_attention",
    )(proj3, proj3, proj3)


def _merge_kernel(x_ref, oa_ref, ob_ref, oc_ref, g_ref, wb_ref, wo_ref, o_ref):
    merged = None
    for n, br in enumerate((oa_ref, ob_ref, oc_ref)):
        y = jnp.dot(br[...], wb_ref[n], preferred_element_type=F32)
        term = jax.nn.sigmoid(g_ref[:, n * D_MODEL:(n + 1) * D_MODEL]) * y
        merged = term if merged is None else merged + term
    o_ref[...] = x_ref[...] + jnp.dot(merged.astype(BF16), wo_ref[...], preferred_element_type=F32)


def _merge(x2, oa, ob, oc, proj2, w_branch, w_out, tm=512):
    t = x2.shape[0]
    assert t % tm == 0
    row = lambda i: (i, 0)
    return pl.pallas_call(
        _merge_kernel,
        grid=(t // tm,),
        in_specs=[
            pl.BlockSpec((tm, D_MODEL), row),
            pl.BlockSpec((tm, BRANCH_WIDTH), row),
            pl.BlockSpec((tm, BRANCH_WIDTH), row),
            pl.BlockSpec((tm, BRANCH_WIDTH), row),
            pl.BlockSpec((tm, N_BRANCH * D_MODEL), lambda i: (i, COL_GATES // (N_BRANCH * D_MODEL))),
            pl.BlockSpec((N_BRANCH, BRANCH_WIDTH, D_MODEL), lambda i: (0, 0, 0)),
            pl.BlockSpec((D_MODEL, D_MODEL), lambda i: (0, 0)),
        ],
        out_specs=pl.BlockSpec((tm, D_MODEL), row),
        out_shape=jax.ShapeDtypeStruct((t, D_MODEL), F32),
        compiler_params=_cparams(1),
        name="gated_merge_out_proj",
    )(x2, oa, ob, oc, proj2, w_branch, w_out)


def _ffn_kernel(x_ref, g_ref, w1_ref, w2_ref, gf_ref, o_ref, h_ref, acc_ref, *, final_norm):
    j = pl.program_id(1)

    @pl.when(j == 0)
    def _():
        x = x_ref[...]
        ms = jnp.mean(x * x, axis=-1, keepdims=True)
        h_ref[...] = (x * lax.rsqrt(ms + NORM_EPS) * g_ref[...]).astype(BF16)
        acc_ref[...] = jnp.zeros(acc_ref.shape, F32)

    u = jnp.maximum(jnp.dot(h_ref[...], w1_ref[...], preferred_element_type=F32), 0.0)
    acc_ref[...] += jnp.dot((u * u).astype(BF16), w2_ref[...], preferred_element_type=F32)

    @pl.when(j == pl.num_programs(1) - 1)
    def _():
        y = x_ref[...] + acc_ref[...]
        if final_norm:
            ms = jnp.mean(y * y, axis=-1, keepdims=True)
            y = y * lax.rsqrt(ms + NORM_EPS) * gf_ref[...]
        o_ref[...] = y


def _ffn(x2, gain, w1, w2, gain_f, final_norm, tm=1024, tf=2048):
    t = x2.shape[0]
    tm = min(tm, t)
    assert t % tm == 0 and D_FF % tf == 0
    kern = functools.partial(_ffn_kernel, final_norm=final_norm)
    return pl.pallas_call(
        kern,
        grid=(t // tm, D_FF // tf),
        in_specs=[
            pl.BlockSpec((tm, D_MODEL), lambda i, j: (i, 0)),
            pl.BlockSpec((1, D_MODEL), lambda i, j: (0, 0)),
            pl.BlockSpec((D_MODEL, tf), lambda i, j: (0, j)),
            pl.BlockSpec((tf, D_MODEL), lambda i, j: (j, 0)),
            pl.BlockSpec((1, D_MODEL), lambda i, j: (0, 0)),
        ],
        out_specs=pl.BlockSpec((tm, D_MODEL), lambda i, j: (i, 0)),
        out_shape=jax.ShapeDtypeStruct((t, D_MODEL), F32),
        scratch_shapes=[pltpu.VMEM((tm, D_MODEL), BF16), pltpu.VMEM((tm, D_MODEL), F32)],
        compiler_params=_cparams(2),
        name="relu2_mlp",
    )(x2, gain, w1, w2, gain_f)


def _pack_w_in(w_in):
    sizes = (A_HEADS * HEAD_DIM, HEAD_DIM, HEAD_DIM, IDX_HEADS * IDX_DIM, IDX_DIM, IDX_HEADS,
             B_HEADS * HEAD_DIM, B_KV_HEADS * HEAD_DIM, B_KV_HEADS * HEAD_DIM,
             C_HEADS * HEAD_DIM, C_HEADS * HEAD_DIM, C_HEADS * HEAD_DIM, N_BRANCH * D_MODEL)
    offs = np.concatenate([[0], np.cumsum(sizes)])
    w_in = w_in.astype(BF16)
    aq, ak, av, iq, ik, iw, bq, bk, bv, cq, ck, cv, gates = [
        w_in[..., offs[n]:offs[n + 1]] for n in range(len(sizes))]
    lead = w_in.shape[:-1]
    pad_iw = jnp.zeros(lead + (LANES - HEAD_DIM - IDX_HEADS,), BF16)
    parts = [gates, aq, bq, cq, ck, iq, bk, ak, ik, av, iw, pad_iw, bv, cv]
    used = sum(p.shape[-1] for p in parts)
    parts.append(jnp.zeros(lead + (N_PACKED - used,), BF16))
    return jnp.concatenate(parts, axis=-1)


def _rope_tables(seq):
    half = HEAD_DIM // 2
    inv = 1.0 / (ROPE_THETA ** (jnp.arange(0, HEAD_DIM, 2, dtype=F32) / HEAD_DIM))
    ang = jnp.arange(seq, dtype=F32)[:, None] * inv[None, :]
    cos, sin = jnp.cos(ang), jnp.sin(ang)
    reps = LANES // HEAD_DIM
    cos_t = jnp.tile(jnp.concatenate([cos, cos], axis=1), (1, reps))
    sin_t = jnp.tile(jnp.concatenate([-sin, sin], axis=1), (1, reps))
    assert cos_t.shape == (seq, LANES) and half * 2 == HEAD_DIM
    return cos_t, sin_t


def kernel(x, norm1, w_in, sinks, w_branch, w_out, norm2, w_ff1, w_ff2, norm_f):
    b, seq, d = x.shape
    depth = w_in.shape[0]
    cos_t, sin_t = _rope_tables(seq)
    w_in_p = _pack_w_in(w_in)
    w_branch_b = w_branch.astype(BF16)
    w_out_b = w_out.astype(BF16)
    w_ff1_b = w_ff1.astype(BF16)
    w_ff2_b = w_ff2.astype(BF16)
    gain_f = norm_f.reshape(1, d)

    x2 = x.reshape(b * seq, d)
    for l in range(depth):
        proj2 = _project(x2, norm1[l].reshape(1, d), w_in_p[l], cos_t, sin_t, seq)
        proj3 = proj2.reshape(b, seq, N_PACKED)
        o_a = _dsa(proj3).reshape(b * seq, BRANCH_WIDTH)
        o_b = _swa(proj3, sinks[l]).reshape(b * seq, BRANCH_WIDTH)
        o_c = _moba(proj3).reshape(b * seq, BRANCH_WIDTH)
        x2 = _merge(x2, o_a, o_b, o_c, proj2, w_branch_b[l], w_out_b[l])
        x2 = _ffn(x2, norm2[l].reshape(1, d), w_ff1_b[l], w_ff2_b[l], gain_f,
                  final_norm=(l == depth - 1))
    return x2.reshape(b, seq, d)
```

```python
import functools

import jax
import jax.numpy as jnp
import numpy as np
from jax import lax
from jax.experimental import pallas as pl
from jax.experimental.pallas import tpu as pltpu

F32 = jnp.float32
BF16 = jnp.bfloat16
I32 = jnp.int32

D_MODEL = 1024
HEAD_DIM = 64
ROPE_THETA = 10000.0
NORM_EPS = 1e-6
A_HEADS = 8
IDX_HEADS = 4
IDX_DIM = 64
IDX_TOPK_MAX = 256
B_HEADS = 8
B_KV_HEADS = 2
WINDOW = 128
C_HEADS = 8
MOBA_BLOCK = 256
MOBA_TOPK = 3
Q_BLOCK = 128
N_BRANCH = 3
BRANCH_WIDTH = 8 * HEAD_DIM
D_FF = 4 * D_MODEL

LANES = 128
NEG_BIG = -1e30
Q_SCALE_LOG2 = (HEAD_DIM ** -0.5) * float(np.log2(np.e))
INT_MIN = -(2 ** 31)
BISECT_UNCHECKED_STEPS = 14

COL_GATES = 0
COL_AQ = 3072
COL_BQ = 3584
COL_CQ = 4096
COL_CK = 4608
COL_IQ = 5120
COL_BK = 5376
COL_AKIK = 5504
ROPE_BEGIN, ROPE_END = 3072, 5632
COL_AVIW = 5632
COL_BV = 5760
COL_CV = 5888
N_PACKED = 6656

VMEM_LIMIT = 56 * 1024 * 1024


def _cparams(n_axes):
    return pltpu.CompilerParams(dimension_semantics=("arbitrary",) * n_axes,
                                vmem_limit_bytes=VMEM_LIMIT)


def _nt_dot(a, b):
    return lax.dot_general(a, b, (((1,), (1,)), ((), ())), preferred_element_type=F32)


def _pipelined(n, produce, consume, init):
    produce(0, 0)

    def pair(t, carry):
        c = 2 * t
        produce(c + 1, 1)
        carry = consume(c, 0, carry, False)
        produce(c + 2, 0)
        return consume(c + 1, 1, carry, False)
    n_pairs = (n - 1) // 2
    carry = lax.fori_loop(0, n_pairs, pair, init)
    c0 = 2 * n_pairs

    def two_left(carry):
        produce(c0 + 1, 1)
        carry = consume(c0, 0, carry, False)
        return consume(c0 + 1, 1, carry, True)

    def one_left(carry):
        return consume(c0, 0, carry, True)
    return lax.cond(n - c0 == 2, two_left, one_left, carry)


def _split3_lhs(x):
    hi = x.astype(BF16)
    lo = (x - hi.astype(F32)).astype(BF16)
    return jnp.concatenate([hi, hi, lo], axis=1)


def _split3_rhs(x):
    hi = x.astype(BF16)
    lo = (x - hi.astype(F32)).astype(BF16)
    return jnp.concatenate([hi, lo, hi], axis=1)


def _proj_kernel(x_ref, g_ref, w_ref, cos_ref, sin_ref, o_ref, h_ref, *, rope_lo, rope_hi, tn):
    j = pl.program_id(1)

    @pl.when(j == 0)
    def _():
        x = x_ref[...]
        ms = jnp.mean(x * x, axis=-1, keepdims=True)
        h_ref[...] = (x * lax.rsqrt(ms + NORM_EPS) * g_ref[...]).astype(BF16)

    acc = jnp.dot(h_ref[...], w_ref[...], preferred_element_type=F32)
    is_rope = jnp.logical_and(j >= rope_lo, j < rope_hi)

    @pl.when(is_rope)
    def _():
        cos = cos_ref[...]
        sin = sin_ref[...]
        lane = lax.broadcasted_iota(I32, cos.shape, 1)
        first_half = (lane % HEAD_DIM) < (HEAD_DIM // 2)
        for c in range(tn // LANES):
            a = acc[:, c * LANES:(c + 1) * LANES]
            partner = jnp.where(first_half,
                                pltpu.roll(a, LANES - HEAD_DIM // 2, 1),
                                pltpu.roll(a, HEAD_DIM // 2, 1))
            o_ref[:, c * LANES:(c + 1) * LANES] = a * cos + partner * sin

    @pl.when(jnp.logical_not(is_rope))
    def _():
        o_ref[...] = acc


def _project(x2, gain, w_packed, cos_t, sin_t, seq, tm=2048, tn=512):
    t = x2.shape[0]
    tm = min(tm, seq)
    n = w_packed.shape[1]
    assert t % tm == 0 and seq % tm == 0 and n % tn == 0
    assert ROPE_BEGIN % tn == 0 and ROPE_END % tn == 0
    pos_blocks = seq // tm
    kern = functools.partial(_proj_kernel, rope_lo=ROPE_BEGIN // tn, rope_hi=ROPE_END // tn, tn=tn)
    return pl.pallas_call(
        kern,
        grid=(t // tm, n // tn),
        in_specs=[
            pl.BlockSpec((tm, D_MODEL), lambda i, j: (i, 0)),
            pl.BlockSpec((1, D_MODEL), lambda i, j: (0, 0)),
            pl.BlockSpec((D_MODEL, tn), lambda i, j: (0, j)),
            pl.BlockSpec((tm, LANES), lambda i, j: (i % pos_blocks, 0)),
            pl.BlockSpec((tm, LANES), lambda i, j: (i % pos_blocks, 0)),
        ],
        out_specs=pl.BlockSpec((tm, tn), lambda i, j: (i, j)),
        out_shape=jax.ShapeDtypeStruct((t, n), F32),
        scratch_shapes=[pltpu.VMEM((tm, D_MODEL), BF16)],
        compiler_params=_cparams(2),
        name="norm_proj_rope",
    )(x2, gain, w_packed, cos_t, sin_t)


def _value_with_ones(v):
    lane = lax.broadcasted_iota(I32, v.shape, 1)
    return jnp.concatenate([v.astype(BF16), jnp.where(lane == 0, 1.0, 0.0).astype(BF16)], axis=1)


def _dsa_kernel(aq_ref, iq_ref, kk_ref, vw_ref, o_ref,
                kbf_ref, vbf_ref, ikc_ref, keys_ref, qall_ref, mrow_ref, acc_ref, tri_ref, ones_ref,
                sbuf_ref, *, seq, kc, qb, top_k):
    i = pl.program_id(1)
    n_prep = seq // kc
    idx_scale = (IDX_DIM ** -0.5) * (IDX_HEADS ** -0.5)

    @pl.when(i == 0)
    def _prep():
        def body(c, carry):
            rows = pl.ds(pl.multiple_of(c * kc, kc), kc)
            kk = kk_ref[rows, :]
            kbf_ref[rows, :] = kk[:, :HEAD_DIM].astype(BF16)
            ikc_ref[rows, :] = _split3_rhs(kk[:, HEAD_DIM:])
            vbf_ref[rows, :] = _value_with_ones(vw_ref[rows, :HEAD_DIM])
            return carry
        lax.fori_loop(0, n_prep, body, 0)
        r = lax.broadcasted_iota(I32, (kc, kc), 0)
        cidx = lax.broadcasted_iota(I32, (kc, kc), 1)
        tri_ref[...] = jnp.where(r > cidx, 1.0, 0.0).astype(BF16)
        ones_ref[...] = jnp.ones(ones_ref.shape, BF16)

    nkc = (i * qb) // kc + 1

    iq = iq_ref[...]
    w = vw_ref[pl.ds(pl.multiple_of(i * qb, qb), qb), HEAD_DIM:HEAD_DIM + IDX_HEADS] * idx_scale
    qcat = jnp.concatenate(
        [_split3_lhs(iq[:, h * IDX_DIM:(h + 1) * IDX_DIM]) for h in range(IDX_HEADS)], axis=0)
    row = lax.broadcasted_iota(I32, (qb, LANES), 0)
    lane = lax.broadcasted_iota(I32, (qb, LANES), 1)
    qpos = i * qb + row
    w_b = [jnp.broadcast_to(w[:, h:h + 1], (qb, LANES)) for h in range(IDX_HEADS)]
    ngrp = kc // LANES

    def chunk_rows(c):
        return pl.ds(pl.multiple_of(c * kc, kc), kc)

    def to_key(x):
        bits = lax.bitcast_convert_type(x, I32)
        return jnp.where(bits < 0, bits ^ jnp.int32(0x7FFFFFFF), bits)

    def idx_dots(c, slot):
        sbuf_ref[slot, :IDX_HEADS * qb] = _nt_dot(qcat, ikc_ref[chunk_rows(c), :])

    def score_chunk(c, slot, smax, is_last):
        for g in range(ngrp):
            sc = None
            for h in range(IDX_HEADS):
                d = sbuf_ref[slot, h * qb:(h + 1) * qb, g * LANES:(g + 1) * LANES]
                term = jnp.maximum(d, 0.0) * w_b[h]
                sc = term if sc is None else sc + term
            sc = jnp.where(sc == 0.0, 0.0, sc)
            key = to_key(sc)
            if is_last:
                causal = c * kc + g * LANES + lane <= qpos
                key = jnp.where(causal, key, jnp.int32(INT_MIN))
                sc = jnp.where(causal, sc, -jnp.inf)
            keys_ref[c, :, g * LANES:(g + 1) * LANES] = key
            smax = jnp.maximum(smax, sc)
        return smax

    smax = _pipelined(nkc, idx_dots, score_chunk, jnp.full((qb, LANES), -jnp.inf, F32))
    key_max = to_key(jnp.max(smax, axis=1, keepdims=True))

    def count(pred, thr):
        parts = []
        for r0 in range(0, qb, LANES):
            thr_b = jnp.broadcast_to(thr[r0:r0 + LANES], (LANES, LANES))

            def body(c, acc, r0=r0, thr_b=thr_b):
                kch = keys_ref[c, r0:r0 + LANES, :]
                for g in range(kc // LANES):
                    acc = acc + jnp.where(pred(kch[:, g * LANES:(g + 1) * LANES], thr_b), 1.0, 0.0)
                return acc
            parts.append(lax.fori_loop(0, nkc, body, jnp.zeros((LANES, LANES), F32)))
        return jnp.sum(jnp.concatenate(parts, axis=0), axis=1, keepdims=True)

    ge = lambda k, t: k >= t
    kf = float(top_k)

    def probe(v, state):
        lo, n_lo, hi = state
        n_v = count(ge, v)
        inside = jnp.logical_and(v > lo, v < hi)
        up = jnp.logical_and(inside, n_v >= kf)
        down = jnp.logical_and(inside, n_v < kf)
        return jnp.where(up, v, lo), jnp.where(up, n_v, n_lo), jnp.where(down, v, hi)

    def bisect(state):
        lo, n_lo, hi = state
        mid = lo + lax.shift_right_logical(hi - lo, jnp.int32(1))
        n_mid = count(ge, mid)
        up = n_mid >= kf
        return jnp.where(up, mid, lo), jnp.where(up, n_mid, n_lo), jnp.where(up, hi, mid)

    def unfinished(state):
        lo, n_lo, hi = state
        done = jnp.logical_or(n_lo == kf, hi - 1 == lo)
        return jnp.max(jnp.where(done, 0.0, 1.0)) > 0.0

    state = (jnp.full((qb, 1), INT_MIN, I32), jnp.full((qb, 1), float(seq + 1), F32), key_max + 1)
    state = probe(jnp.ones((qb, 1), I32), state)
    state = probe(jnp.zeros((qb, 1), I32), state)
    two_binades = jnp.int32(1 << 24)
    state = probe(jnp.maximum(key_max, jnp.int32(INT_MIN) + two_binades) - two_binades, state)
    state = lax.fori_loop(0, BISECT_UNCHECKED_STEPS, lambda _, st: bisect(st), state)

    def bisect_cond(carry):
        return jnp.logical_and(carry[0], carry[1] < 34)

    def bisect_body(carry):
        state = bisect(carry[2:])
        return (unfinished(state), carry[1] + 1) + state
    carry = lax.while_loop(bisect_cond, bisect_body, (unfinished(state), jnp.int32(0)) + state)
    thr, n_thr = carry[2], carry[3]

    excess = jnp.where(thr > jnp.int32(INT_MIN), n_thr - kf, float(seq + 1))
    thr_b = jnp.broadcast_to(thr, (qb, LANES))
    excess_b = jnp.broadcast_to(excess, (qb, LANES))

    aq = aq_ref[...]
    for h in range(A_HEADS):
        qall_ref[h * qb:(h + 1) * qb, :] = (
            aq[:, h * HEAD_DIM:(h + 1) * HEAD_DIM] * Q_SCALE_LOG2).astype(BF16)

    def logits(c, slot):
        sbuf_ref[slot] = _nt_dot(qall_ref[...], kbf_ref[chunk_rows(c), :])

    def later_ties(step, slot):
        kch = keys_ref[nkc - 1 - step]
        eq_b = jnp.concatenate(
            [jnp.where(kch[:, g * LANES:(g + 1) * LANES] == thr_b, 1.0, 0.0).astype(BF16)
             for g in range(ngrp)], axis=1)
        sbuf_ref[slot, :qb] = jnp.dot(eq_b, tri_ref[...], preferred_element_type=F32)
        sbuf_ref[slot, qb:2 * qb, :LANES] = jnp.dot(eq_b, ones_ref[...], preferred_element_type=F32)

    def bias_chunk(step, slot, ties_after, is_last):
        c = nkc - 1 - step
        kch = keys_ref[c]
        groups = []
        for g in range(ngrp):
            k_g = kch[:, g * LANES:(g + 1) * LANES]
            later = sbuf_ref[slot, :qb, g * LANES:(g + 1) * LANES] + ties_after
            keep = jnp.logical_or(k_g > thr_b, jnp.logical_and(k_g == thr_b, later >= excess_b))
            groups.append(jnp.where(keep, 0.0, NEG_BIG))
        keys_ref[c] = lax.bitcast_convert_type(jnp.concatenate(groups, axis=1), I32)
        return ties_after + sbuf_ref[slot, qb:2 * qb, :LANES]
    _pipelined(nkc, later_ties, bias_chunk, jnp.zeros((qb, LANES), F32))

    def masked(slot, bias):
        return (sbuf_ref[slot].reshape(A_HEADS, qb, kc) + bias[None]).reshape(A_HEADS * qb, kc)

    def att_chunk(c, slot, carry, is_last):
        s = masked(slot, lax.bitcast_convert_type(keys_ref[c], F32))
        m_old = mrow_ref[...]
        mx = s[:, :LANES]
        for g in range(1, ngrp):
            mx = jnp.maximum(mx, s[:, g * LANES:(g + 1) * LANES])
        m_new = jnp.maximum(m_old, jnp.broadcast_to(jnp.max(mx, axis=1, keepdims=True), m_old.shape))
        p = jnp.concatenate(
            [jnp.exp2(s[:, g * LANES:(g + 1) * LANES] - m_new).astype(BF16) for g in range(ngrp)], axis=1)
        acc_ref[...] = jnp.exp2(m_old - m_new) * acc_ref[...] + jnp.dot(
            p, vbf_ref[chunk_rows(c), :], preferred_element_type=F32)
        mrow_ref[...] = m_new
        return carry
    mrow_ref[...] = jnp.full(mrow_ref.shape, NEG_BIG, F32)
    acc_ref[...] = jnp.zeros(acc_ref.shape, F32)
    _pipelined(nkc, logits, att_chunk, 0)

    acc = acc_ref[...]
    out = acc[:, :HEAD_DIM] / acc[:, HEAD_DIM:HEAD_DIM + 1]
    for h in range(A_HEADS):
        o_ref[:, h * HEAD_DIM:(h + 1) * HEAD_DIM] = out[h * qb:(h + 1) * qb].astype(o_ref.dtype)


def _dsa(proj3, kc=512, qb=256):
    b, seq, _ = proj3.shape
    top_k = min(IDX_TOPK_MAX, seq // 4)
    kc = min(kc, seq)
    assert seq % kc == 0 and kc % qb == 0 and seq % qb == 0 and kc >= top_k
    kern = functools.partial(_dsa_kernel, seq=seq, kc=kc, qb=qb, top_k=top_k)
    hq = A_HEADS * qb
    return pl.pallas_call(
        kern,
        grid=(b, seq // qb),
        in_specs=[
            pl.BlockSpec((None, qb, A_HEADS * HEAD_DIM), lambda bb, i: (bb, i, COL_AQ // 512)),
            pl.BlockSpec((None, qb, IDX_HEADS * IDX_DIM), lambda bb, i: (bb, i, COL_IQ // 256)),
            pl.BlockSpec((None, seq, LANES), lambda bb, i: (bb, 0, COL_AKIK // LANES),
                         pipeline_mode=pl.Buffered(1)),
            pl.BlockSpec((None, seq, LANES), lambda bb, i: (bb, 0, COL_AVIW // LANES),
                         pipeline_mode=pl.Buffered(1)),
        ],
        out_specs=pl.BlockSpec((None, qb, A_HEADS * HEAD_DIM), lambda bb, i: (bb, i, 0)),
        out_shape=jax.ShapeDtypeStruct((b, seq, A_HEADS * HEAD_DIM), BF16),
        scratch_shapes=[
            pltpu.VMEM((seq, HEAD_DIM), BF16),
            pltpu.VMEM((seq, LANES), BF16),
            pltpu.VMEM((seq, 3 * IDX_DIM), BF16),
            pltpu.VMEM((seq // kc, qb, kc), I32),
            pltpu.VMEM((hq, HEAD_DIM), BF16),
            pltpu.VMEM((hq, LANES), F32),
            pltpu.VMEM((hq, LANES), F32),
            pltpu.VMEM((kc, kc), BF16),
            pltpu.VMEM((kc, LANES), BF16),
            pltpu.VMEM((2, hq, kc), F32),
        ],
        compiler_params=_cparams(2),
        name="dsa_topk_attention",
    )(proj3, proj3, proj3, proj3)


def _swa_kernel(sink_ref, q_ref, kp_ref, kc_ref, vp_ref, vc_ref, o_ref):
    i = pl.program_id(1)
    w = WINDOW
    group = B_HEADS // B_KV_HEADS
    q = q_ref[...]
    k = jnp.concatenate([kp_ref[...], kc_ref[...]], axis=0)
    v = jnp.concatenate([vp_ref[...], vc_ref[...]], axis=0)
    srow = lax.broadcasted_iota(I32, (group * w, 2 * w), 0)
    col = lax.broadcasted_iota(I32, (group * w, 2 * w), 1)
    kpos = (i - 1) * w + col
    diff = i * w + (srow % w) - kpos
    mask = jnp.logical_and(jnp.logical_and(diff >= 0, diff < w), kpos >= 0)
    head_of_row = lax.broadcasted_iota(I32, (group * w, 1), 0) // w
    for g in range(B_KV_HEADS):
        kg = k[:, g * HEAD_DIM:(g + 1) * HEAD_DIM].astype(BF16)
        vg = v[:, g * HEAD_DIM:(g + 1) * HEAD_DIM].astype(BF16)
        heads = [g * group + hh for hh in range(group)]
        qg = jnp.concatenate(
            [(q[:, h * HEAD_DIM:(h + 1) * HEAD_DIM] * (HEAD_DIM ** -0.5)).astype(BF16) for h in heads],
            axis=0)
        sink = jnp.zeros((group * w, 1), F32)
        for hh, h in enumerate(heads):
            sink = jnp.where(head_of_row == hh, sink_ref[h], sink)
        s = jnp.where(mask, _nt_dot(qg, kg), NEG_BIG)
        m = jnp.maximum(jnp.max(s, axis=1, keepdims=True), sink)
        p = jnp.exp(s - m)
        den = jnp.sum(p, axis=1, keepdims=True) + jnp.exp(sink - m)
        o = jnp.dot(p.astype(BF16), vg, preferred_element_type=F32) / den
        for hh, h in enumerate(heads):
            o_ref[:, h * HEAD_DIM:(h + 1) * HEAD_DIM] = o[hh * w:(hh + 1) * w].astype(o_ref.dtype)


def _swa(proj3, sinks):
    b, seq, _ = proj3.shape
    w = WINDOW
    kvw = B_KV_HEADS * HEAD_DIM
    assert kvw == LANES and seq % w == 0
    prev = lambda col: (lambda bb, i: (bb, jnp.maximum(i - 1, 0), col))
    cur = lambda col: (lambda bb, i: (bb, i, col))
    return pl.pallas_call(
        _swa_kernel,
        grid=(b, seq // w),
        in_specs=[
            pl.BlockSpec(memory_space=pltpu.SMEM),
            pl.BlockSpec((None, w, B_HEADS * HEAD_DIM), cur(COL_BQ // 512)),
            pl.BlockSpec((None, w, kvw), prev(COL_BK // LANES)),
            pl.BlockSpec((None, w, kvw), cur(COL_BK // LANES)),
            pl.BlockSpec((None, w, kvw), prev(COL_BV // LANES)),
            pl.BlockSpec((None, w, kvw), cur(COL_BV // LANES)),
        ],
        out_specs=pl.BlockSpec((None, w, B_HEADS * HEAD_DIM), lambda bb, i: (bb, i, 0)),
        out_shape=jax.ShapeDtypeStruct((b, seq, B_HEADS * HEAD_DIM), BF16),
        compiler_params=_cparams(2),
        name="swa_sink_attention",
    )(sinks, proj3, proj3, proj3, proj3, proj3)


def _moba_kernel(q_ref, k_ref, v_ref, o_ref,
                 kaug_ref, vaug_ref, kmean_ref, kmw_ref, qaug_ref, mrow_ref, acc_ref, sbuf_ref,
                 *, seq, tq, n_sel):
    i = pl.program_id(2)
    blk = MOBA_BLOCK
    nkb = seq // blk
    heads = LANES // HEAD_DIM
    assert heads == 2
    ones_lane = (HEAD_DIM, 0)

    @pl.when(i == 0)
    def _prep():
        kmean_ref[...] = jnp.zeros(kmean_ref.shape, F32)
        lane_b = lax.broadcasted_iota(I32, (blk, LANES), 1)

        def body(n, carry):
            rows = pl.ds(pl.multiple_of(n * blk, blk), blk)
            kb = k_ref[rows, :]
            vb = v_ref[rows, :]
            onehot = jnp.where(lane_b == n, 1.0, 0.0).astype(BF16)
            for e in range(heads):
                mine = (lane_b // HEAD_DIM) == e
                kaug_ref[e, rows, :] = jnp.concatenate(
                    [onehot, jnp.where(mine, kb, 0.0).astype(BF16)], axis=1)
                ones = jnp.where(lane_b == ones_lane[e], 1.0, 0.0)
                vaug_ref[e, rows, :] = jnp.where(mine, vb, ones).astype(BF16)
            kmean_ref[pl.ds(n, 1), :] = jnp.sum(kb, axis=0, keepdims=True) * (1.0 / blk)
            return carry
        lax.fori_loop(0, nkb, body, 0)
        km = kmean_ref[...]
        lane_m = lax.broadcasted_iota(I32, km.shape, 1)
        kmw = jnp.concatenate(
            [jnp.where((lane_m // HEAD_DIM) == e, km, 0.0) for e in range(heads)], axis=0)
        hi = kmw.astype(BF16)
        kmw_ref[0] = hi
        kmw_ref[1] = (kmw - hi.astype(F32)).astype(BF16)

    q = q_ref[...]
    lane = lax.broadcasted_iota(I32, (tq, LANES), 1)

    qhi = q.astype(BF16)
    qlo = (q - qhi.astype(F32)).astype(BF16)
    gates_t = (_nt_dot(kmw_ref[0], qhi) + _nt_dot(kmw_ref[1], qhi)) + _nt_dot(kmw_ref[0], qlo)
    q2 = (q * Q_SCALE_LOG2).astype(BF16)
    nb8 = -(-nkb // 8) * 8
    own_t = (i * tq + lax.broadcasted_iota(I32, (nb8, tq), 1)) // blk
    blk_t = lax.broadcasted_iota(I32, (nb8, tq), 0)
    blk_tf = blk_t.astype(F32)
    for e in range(heads):
        g = jnp.where(blk_t < own_t, gates_t[e * LANES:e * LANES + nb8], -jnp.inf)
        allowed = blk_t == own_t
        for _ in range(n_sel):
            gmax = jnp.max(g, axis=0, keepdims=True)
            is_max = jnp.logical_and(g == gmax, gmax > -jnp.inf)
            first = jnp.min(jnp.where(is_max, blk_tf, float(LANES)), axis=0, keepdims=True)
            onehot = blk_tf == first
            allowed = jnp.logical_or(allowed, onehot)
            g = jnp.where(onehot, -jnp.inf, g)
        bias_t = jnp.where(allowed, 0.0, NEG_BIG)
        if nb8 < LANES:
            bias_t = jnp.concatenate([bias_t, jnp.full((LANES - nb8, tq), NEG_BIG, F32)], axis=0)
        qaug_ref[e] = jnp.concatenate([bias_t.T.astype(BF16), q2], axis=1)

    kc = tq
    ngrp = kc // LANES
    col = lax.broadcasted_iota(I32, (tq, kc), 1)
    row = lax.broadcasted_iota(I32, (tq, kc), 0)

    def chunk_rows(c):
        return pl.ds(pl.multiple_of(c * kc, kc), kc)

    def logits(c, slot):
        for e in range(heads):
            sbuf_ref[slot, e] = _nt_dot(qaug_ref[e], kaug_ref[e, chunk_rows(c), :])

    def masked(e, slot, diagonal):
        s = sbuf_ref[slot, e]
        return jnp.where(col <= row, s, NEG_BIG) if diagonal else s

    def att_chunk(c, slot, carry, is_last):
        for e in range(heads):
            s = masked(e, slot, is_last)
            m_old = mrow_ref[e]
            mx = s[:, :LANES]
            for gi in range(1, ngrp):
                mx = jnp.maximum(mx, s[:, gi * LANES:(gi + 1) * LANES])
            m_new = jnp.maximum(m_old, jnp.broadcast_to(jnp.max(mx, axis=1, keepdims=True), (tq, LANES)))
            p = jnp.concatenate(
                [jnp.exp2(s[:, gi * LANES:(gi + 1) * LANES] - m_new).astype(BF16) for gi in range(ngrp)],
                axis=1)
            acc_ref[e] = jnp.exp2(m_old - m_new) * acc_ref[e] + jnp.dot(
                p, vaug_ref[e, chunk_rows(c), :], preferred_element_type=F32)
            mrow_ref[e] = m_new
        return carry
    mrow_ref[...] = jnp.full(mrow_ref.shape, NEG_BIG, F32)
    acc_ref[...] = jnp.zeros(acc_ref.shape, F32)
    _pipelined(i + 1, logits, att_chunk, 0)

    outs = []
    for e in range(heads):
        acc = acc_ref[e]
        outs.append(acc / acc[:, ones_lane[e]:ones_lane[e] + 1])
    o_ref[...] = jnp.where(lane < HEAD_DIM, outs[0], outs[1]).astype(o_ref.dtype)


def _moba(proj3, tq=1024):
    b, seq, _ = proj3.shape
    tq = min(tq, seq)
    blk = MOBA_BLOCK
    assert seq % blk == 0 and tq % blk == 0 and seq % tq == 0
    nkb = seq // blk
    assert nkb <= LANES
    n_sel = min(MOBA_TOPK, nkb - 1)
    heads = LANES // HEAD_DIM
    kern = functools.partial(_moba_kernel, seq=seq, tq=tq, n_sel=n_sel)
    return pl.pallas_call(
        kern,
        grid=(b, C_HEADS // heads, seq // tq),
        in_specs=[
            pl.BlockSpec((None, tq, LANES), lambda bb, hp, i: (bb, i, COL_CQ // LANES + hp)),
            pl.BlockSpec((None, seq, LANES), lambda bb, hp, i: (bb, 0, COL_CK // LANES + hp),
                         pipeline_mode=pl.Buffered(1)),
            pl.BlockSpec((None, seq, LANES), lambda bb, hp, i: (bb, 0, COL_CV // LANES + hp),
                         pipeline_mode=pl.Buffered(1)),
        ],
        out_specs=pl.BlockSpec((None, tq, LANES), lambda bb, hp, i: (bb, i, hp)),
        out_shape=jax.ShapeDtypeStruct((b, seq, C_HEADS * HEAD_DIM), BF16),
        scratch_shapes=[
            pltpu.VMEM((heads, seq, 2 * LANES), BF16),
            pltpu.VMEM((heads, seq, LANES), BF16),
            pltpu.VMEM((LANES, LANES), F32),
            pltpu.VMEM((2, heads * LANES, LANES), BF16),
            pltpu.VMEM((heads, tq, 2 * LANES), BF16),
            pltpu.VMEM((heads, tq, LANES), F32),
            pltpu.VMEM((heads, tq, LANES), F32),
            pltpu.VMEM((2, heads, tq, tq), F32),
        ],
        compiler_params=_cparams(3),
        name="moba_attention",
    )(proj3, proj3, proj3)


def _merge_kernel(x_ref, oa_ref, ob_ref, oc_ref, g_ref, wb_ref, wo_ref, o_ref):
    merged = None
    for n, br in enumerate((oa_ref, ob_ref, oc_ref)):
        y = jnp.dot(br[...], wb_ref[n], preferred_element_type=F32)
        term = jax.nn.sigmoid(g_ref[:, n * D_MODEL:(n + 1) * D_MODEL]) * y
        merged = term if merged is None else merged + term
    o_ref[...] = x_ref[...] + jnp.dot(merged.astype(BF16), wo_ref[...], preferred_element_type=F32)


def _merge(x2, oa, ob, oc, proj2, w_branch, w_out, tm=512):
    t = x2.shape[0]
    assert t % tm == 0
    row = lambda i: (i, 0)
    return pl.pallas_call(
        _merge_kernel,
        grid=(t // tm,),
        in_specs=[
            pl.BlockSpec((tm, D_MODEL), row),
            pl.BlockSpec((tm, BRANCH_WIDTH), row),
            pl.BlockSpec((tm, BRANCH_WIDTH), row),
            pl.BlockSpec((tm, BRANCH_WIDTH), row),
            pl.BlockSpec((tm, N_BRANCH * D_MODEL), lambda i: (i, COL_GATES // (N_BRANCH * D_MODEL))),
            pl.BlockSpec((N_BRANCH, BRANCH_WIDTH, D_MODEL), lambda i: (0, 0, 0)),
            pl.BlockSpec((D_MODEL, D_MODEL), lambda i: (0, 0)),
        ],
        out_specs=pl.BlockSpec((tm, D_MODEL), row),
        out_shape=jax.ShapeDtypeStruct((t, D_MODEL), F32),
        compiler_params=_cparams(1),
        name="gated_merge_out_proj",
    )(x2, oa, ob, oc, proj2, w_branch, w_out)


def _ffn_kernel(x_ref, g_ref, w1_ref, w2_ref, gf_ref, o_ref, h_ref, acc_ref, *, final_norm):
    j = pl.program_id(1)

    @pl.when(j == 0)
    def _():
        x = x_ref[...]
        ms = jnp.mean(x * x, axis=-1, keepdims=True)
        h_ref[...] = (x * lax.rsqrt(ms + NORM_EPS) * g_ref[...]).astype(BF16)
        acc_ref[...] = jnp.zeros(acc_ref.shape, F32)

    u = jnp.maximum(jnp.dot(h_ref[...], w1_ref[...], preferred_element_type=F32), 0.0)
    acc_ref[...] += jnp.dot((u * u).astype(BF16), w2_ref[...], preferred_element_type=F32)

    @pl.when(j == pl.num_programs(1) - 1)
    def _():
        y = x_ref[...] + acc_ref[...]
        if final_norm:
            ms = jnp.mean(y * y, axis=-1, keepdims=True)
            y = y * lax.rsqrt(ms + NORM_EPS) * gf_ref[...]
        o_ref[...] = y


def _ffn(x2, gain, w1, w2, gain_f, final_norm, tm=1024, tf=2048):
    t = x2.shape[0]
    tm = min(tm, t)
    assert t % tm == 0 and D_FF % tf == 0
    kern = functools.partial(_ffn_kernel, final_norm=final_norm)
    return pl.pallas_call(
        kern,
        grid=(t // tm, D_FF // tf),
        in_specs=[
            pl.BlockSpec((tm, D_MODEL), lambda i, j: (i, 0)),
            pl.BlockSpec((1, D_MODEL), lambda i, j: (0, 0)),
            pl.BlockSpec((D_MODEL, tf), lambda i, j: (0, j)),
            pl.BlockSpec((tf, D_MODEL), lambda i, j: (j, 0)),
            pl.BlockSpec((1, D_MODEL), lambda i, j: (0, 0)),
        ],
        out_specs=pl.BlockSpec((tm, D_MODEL), lambda i, j: (i, 0)),
        out_shape=jax.ShapeDtypeStruct((t, D_MODEL), F32),
        scratch_shapes=[pltpu.VMEM((tm, D_MODEL), BF16), pltpu.VMEM((tm, D_MODEL), F32)],
        compiler_params=_cparams(2),
        name="relu2_mlp",
    )(x2, gain, w1, w2, gain_f)


def _pack_w_in(w_in):
    sizes = (A_HEADS * HEAD_DIM, HEAD_DIM, HEAD_DIM, IDX_HEADS * IDX_DIM, IDX_DIM, IDX_HEADS,
             B_HEADS * HEAD_DIM, B_KV_HEADS * HEAD_DIM, B_KV_HEADS * HEAD_DIM,
             C_HEADS * HEAD_DIM, C_HEADS * HEAD_DIM, C_HEADS * HEAD_DIM, N_BRANCH * D_MODEL)
    offs = np.concatenate([[0], np.cumsum(sizes)])
    w_in = w_in.astype(BF16)
    aq, ak, av, iq, ik, iw, bq, bk, bv, cq, ck, cv, gates = [
        w_in[..., offs[n]:offs[n + 1]] for n in range(len(sizes))]
    lead = w_in.shape[:-1]
    pad_iw = jnp.zeros(lead + (LANES - HEAD_DIM - IDX_HEADS,), BF16)
    parts = [gates, aq, bq, cq, ck, iq, bk, ak, ik, av, iw, pad_iw, bv, cv]
    used = sum(p.shape[-1] for p in parts)
    parts.append(jnp.zeros(lead + (N_PACKED - used,), BF16))
    return jnp.concatenate(parts, axis=-1)


def _rope_tables(seq):
    half = HEAD_DIM // 2
    inv = 1.0 / (ROPE_THETA ** (jnp.arange(0, HEAD_DIM, 2, dtype=F32) / HEAD_DIM))
    ang = jnp.arange(seq, dtype=F32)[:, None] * inv[None, :]
    cos, sin = jnp.cos(ang), jnp.sin(ang)
    reps = LANES // HEAD_DIM
    cos_t = jnp.tile(jnp.concatenate([cos, cos], axis=1), (1, reps))
    sin_t = jnp.tile(jnp.concatenate([-sin, sin], axis=1), (1, reps))
    assert cos_t.shape == (seq, LANES) and half * 2 == HEAD_DIM
    return cos_t, sin_t


def kernel(x, norm1, w_in, sinks, w_branch, w_out, norm2, w_ff1, w_ff2, norm_f):
    b, seq, d = x.shape
    depth = w_in.shape[0]
    cos_t, sin_t = _rope_tables(seq)
    w_in_p = _pack_w_in(w_in)
    w_branch_b = w_branch.astype(BF16)
    w_out_b = w_out.astype(BF16)
    w_ff1_b = w_ff1.astype(BF16)
    w_ff2_b = w_ff2.astype(BF16)
    gain_f = norm_f.reshape(1, d)

    x2 = x.reshape(b * seq, d)
    for l in range(depth):
        proj2 = _project(x2, norm1[l].reshape(1, d), w_in_p[l], cos_t, sin_t, seq)
        proj3 = proj2.reshape(b, seq, N_PACKED)
        o_a = _dsa(proj3).reshape(b * seq, BRANCH_WIDTH)
        o_b = _swa(proj3, sinks[l]).reshape(b * seq, BRANCH_WIDTH)
        o_c = _moba(proj3).reshape(b * seq, BRANCH_WIDTH)
        x2 = _merge(x2, o_a, o_b, o_c, proj2, w_branch_b[l], w_out_b[l])
        x2 = _ffn(x2, norm2[l].reshape(1, d), w_ff1_b[l], w_ff2_b[l], gain_f,
                  final_norm=(l == depth - 1))
    return x2.reshape(b, seq, d)
```
